```python
import jax, jax.numpy as jnp
from jax import lax
import numpy as np

D_MODEL = 1024
BATCH = 2
SEQ = 8192
DEPTH = 1

CHUNK = 64
Q_BLOCK = 128
EPS = 1e-6
D_FF = 2816
N_MOD = 9
CONV_WIDTH = 512
CONV_GROUPS = 8
CONV_K = 3
MLA_HEADS = 4
QK_NOPE = 128
QK_ROPE = 64
V_HEAD = 128
Q_LORA = 384
KV_LORA = 256
ROPE_THETA = 10000.0
MLA_WIDTH = MLA_HEADS * V_HEAD
MIX_WIDTH = CONV_WIDTH + MLA_WIDTH
IN_COLS = 3 * CONV_WIDTH + Q_LORA + KV_LORA + QK_ROPE

kernel_name = "hymba_conv_mla_macaron_adaln_block"


def rmsnorm(x, g):
    xf = x.astype(jnp.float32)
    y = xf * lax.rsqrt(jnp.mean(xf * xf, axis=-1, keepdims=True) + EPS)
    return (y * g.astype(jnp.float32)).astype(x.dtype)


def group_rmsnorm(y, g, n_groups):
    b, s, w = y.shape
    yf = y.astype(jnp.float32).reshape(b, s, n_groups, w // n_groups)
    yf = yf * lax.rsqrt(jnp.mean(yf * yf, axis=-1, keepdims=True) + EPS)
    return (yf.reshape(b, s, w) * g.astype(jnp.float32)).astype(y.dtype)


def modulate(h, shift, scale):
    return h * (1.0 + scale[:, None, :]) + shift[:, None, :]


def swiglu(h, w1, w3, w2):
    return (jax.nn.silu(h @ w1) * (h @ w3)) @ w2


def rope(x, cos, sin):
    half = x.shape[-1] // 2
    x1, x2 = x[..., :half], x[..., half:]
    return jnp.concatenate([x1 * cos - x2 * sin, x1 * sin + x2 * cos], axis=-1)


def short_conv_mixer(xb, xc, xu, conv_w):
    u = xc * xu
    s = u.shape[1]
    up = jnp.pad(u, ((0, 0), (CONV_K - 1, 0), (0, 0)))
    y = conv_w[0] * up[:, 0:s]
    for k in range(1, CONV_K):
        y = y + conv_w[k] * up[:, k:k + s]
    return xb * y


def mla(cq, ckv, kr, positions, q_norm_g, w_uq, kv_norm_g, w_ukv):
    b, s, _ = cq.shape
    q = (rmsnorm(cq, q_norm_g) @ w_uq).reshape(b, s, MLA_HEADS, QK_NOPE + QK_ROPE)
    q_nope, q_rope = q[..., :QK_NOPE], q[..., QK_NOPE:]
    kv = (rmsnorm(ckv, kv_norm_g) @ w_ukv).reshape(b, s, MLA_HEADS, QK_NOPE + V_HEAD)
    k_nope, v = kv[..., :QK_NOPE], kv[..., QK_NOPE:]

    inv_freq = ROPE_THETA ** (-jnp.arange(0, QK_ROPE, 2, dtype=jnp.float32) / QK_ROPE)
    ang = positions.astype(jnp.float32)[..., None] * inv_freq
    cos = jnp.cos(ang).astype(cq.dtype)
    sin = jnp.sin(ang).astype(cq.dtype)
    q_rope = rope(q_rope, cos[:, :, None, :], sin[:, :, None, :])
    k_rope = rope(kr, cos, sin)

    scale = (QK_NOPE + QK_ROPE) ** -0.5
    nblk = s // Q_BLOCK
    qn_b = q_nope.reshape(b, nblk, Q_BLOCK, MLA_HEADS, QK_NOPE).transpose(1, 0, 2, 3, 4)
    qr_b = q_rope.reshape(b, nblk, Q_BLOCK, MLA_HEADS, QK_ROPE).transpose(1, 0, 2, 3, 4)
    k_chunk = jnp.arange(s) // CHUNK

    def block(args):
        i, qn, qr = args
        sc = (jnp.einsum('bqhd,bkhd->bhqk', qn, k_nope)
              + jnp.einsum('bqhd,bkd->bhqk', qr, k_rope)).astype(jnp.float32) * scale
        q_chunk = (i * Q_BLOCK + jnp.arange(Q_BLOCK)) // CHUNK
        mask = k_chunk[None, :] <= q_chunk[:, None]
        sc = jnp.where(mask[None, None], sc, jnp.float32(-1e30))
        p = jax.nn.softmax(sc, axis=-1).astype(v.dtype)
        return jnp.einsum('bhqk,bkhd->bqhd', p, v)

    o = lax.map(block, (jnp.arange(nblk), qn_b, qr_b))
    return o.transpose(1, 0, 2, 3, 4).reshape(b, s, MLA_WIDTH)


def setup_inputs(seed: int = 0) -> dict:
    key = jax.random.key(seed)
    ks = iter(jax.random.split(key, 32))
    f32 = jnp.float32
    L = DEPTH

    def nrm(shape, fan_in, scale=1.0):
        return jax.random.normal(next(ks), shape, f32) * (scale * fan_in ** -0.5)

    def gain(shape):
        return 1.0 + 0.05 * jax.random.normal(next(ks), shape, f32)

    x = jax.random.normal(next(ks), (BATCH, SEQ, D_MODEL), f32)
    c = jax.random.normal(next(ks), (BATCH, D_MODEL), f32)
    offset = jax.random.randint(next(ks), (BATCH, 1), 0, 4096)
    positions = (offset + jnp.arange(SEQ)[None, :]).astype(jnp.int32)
    return {
        "x": x,
        "c": c,
        "positions": positions,
        "ada_w": nrm((L, D_MODEL, N_MOD * D_MODEL), D_MODEL, 0.5),
        "ada_b": 0.02 * jax.random.normal(next(ks), (L, N_MOD * D_MODEL), f32),
        "norm_ffn1_g": gain((L, D_MODEL)),
        "ffn1_w1": nrm((L, D_MODEL, D_FF), D_MODEL),
        "ffn1_w3": nrm((L, D_MODEL, D_FF), D_MODEL),
        "ffn1_w2": nrm((L, D_FF, D_MODEL), D_FF),
        "norm_mix_g": gain((L, D_MODEL)),
        "w_in": nrm((L, D_MODEL, IN_COLS), D_MODEL),
        "conv_w": nrm((L, CONV_K, CONV_WIDTH), CONV_K),
        "q_norm_g": gain((L, Q_LORA)),
        "w_uq": nrm((L, Q_LORA, MLA_HEADS * (QK_NOPE + QK_ROPE)), Q_LORA),
        "kv_norm_g": gain((L, KV_LORA)),
        "w_ukv": nrm((L, KV_LORA, MLA_HEADS * (QK_NOPE + V_HEAD)), KV_LORA),
        "out_norm_g": gain((L, MIX_WIDTH)),
        "w_out": nrm((L, MIX_WIDTH, D_MODEL), MIX_WIDTH),
        "norm_ffn2_g": gain((L, D_MODEL)),
        "ffn2_w1": nrm((L, D_MODEL, D_FF), D_MODEL),
        "ffn2_w3": nrm((L, D_MODEL, D_FF), D_MODEL),
        "ffn2_w2": nrm((L, D_FF, D_MODEL), D_FF),
        "final_norm_g": gain((D_MODEL,)),
    }


def reference(x, c, positions, ada_w, ada_b, norm_ffn1_g, ffn1_w1, ffn1_w3, ffn1_w2,
              norm_mix_g, w_in, conv_w, q_norm_g, w_uq, kv_norm_g, w_ukv, out_norm_g,
              w_out, norm_ffn2_g, ffn2_w1, ffn2_w3, ffn2_w2, final_norm_g):
    b = x.shape[0]
    cuts = [CONV_WIDTH, 2 * CONV_WIDTH, 3 * CONV_WIDTH,
            3 * CONV_WIDTH + Q_LORA, 3 * CONV_WIDTH + Q_LORA + KV_LORA]
    for l in range(DEPTH):
        mod = (jax.nn.silu(c) @ ada_w[l] + ada_b[l]).reshape(b, N_MOD, D_MODEL)
        sh1, sc1, g1 = mod[:, 0], mod[:, 1], mod[:, 2]
        sh2, sc2, g2 = mod[:, 3], mod[:, 4], mod[:, 5]
        sh3, sc3, g3 = mod[:, 6], mod[:, 7], mod[:, 8]

        h = modulate(rmsnorm(x, norm_ffn1_g[l]), sh1, sc1)
        x = x + 0.5 * g1[:, None, :] * swiglu(h, ffn1_w1[l], ffn1_w3[l], ffn1_w2[l])

        h = modulate(rmsnorm(x, norm_mix_g[l]), sh2, sc2)
        z = h @ w_in[l]
        xb, xc, xu, cq, ckv, kr = jnp.split(z, cuts, axis=-1)
        y_a = short_conv_mixer(xb, xc, xu, conv_w[l])
        y_b = mla(cq, ckv, kr, positions, q_norm_g[l], w_uq[l], kv_norm_g[l], w_ukv[l])
        y_a = group_rmsnorm(y_a, out_norm_g[l, :CONV_WIDTH], CONV_GROUPS)
        y_b = group_rmsnorm(y_b, out_norm_g[l, CONV_WIDTH:], MLA_HEADS)
        y = jnp.concatenate([y_a, y_b], axis=-1) @ w_out[l]
        x = x + g2[:, None, :] * y

        h = modulate(rmsnorm(x, norm_ffn2_g[l]), sh3, sc3)
        x = x + 0.5 * g3[:, None, :] * swiglu(h, ffn2_w1[l], ffn2_w3[l], ffn2_w2[l])
    return rmsnorm(x, final_norm_g)
```

```python
import functools

import jax
import jax.numpy as jnp
from jax import lax
from jax.experimental import pallas as pl
from jax.experimental.pallas import tpu as pltpu

CHUNK = 64
EPS = 1e-6
N_MOD = 9
CONV_WIDTH = 512
CONV_GROUPS = 8
CONV_K = 3
MLA_HEADS = 4
QK_NOPE = 128
QK_ROPE = 64
V_HEAD = 128
Q_LORA = 384
KV_LORA = 256
ROPE_THETA = 10000.0
MLA_WIDTH = MLA_HEADS * V_HEAD

LANES = 128
SUBLANES = 8
QK_PAD = 256
VMEM_LIMIT_BYTES = 56 * 1024 * 1024

ROW_TILE = 512
ATTN_TQ = 512
MOD_TN = 1024

F32 = jnp.float32
BF16 = jnp.bfloat16


def _params(n_axes=1):
    return pltpu.CompilerParams(
        dimension_semantics=("arbitrary",) * n_axes,
        vmem_limit_bytes=VMEM_LIMIT_BYTES,
    )


def _resident(shape):
    nd = len(shape)
    return pl.BlockSpec(shape, lambda *_: (0,) * nd, pipeline_mode=pl.Buffered(1))


def _rms_scale(x):
    return lax.rsqrt(jnp.mean(x * x, axis=-1, keepdims=True) + EPS)


def _mod_kernel(cb_ref, w_ref, b_ref, o_ref):
    w = w_ref[...]
    tn = w.shape[1]
    for b in range(cb_ref.shape[0]):
        cb = cb_ref[b]
        s = cb * jax.nn.sigmoid(cb)
        cols = [jnp.sum(w[:, j * LANES:(j + 1) * LANES] * s, axis=0, keepdims=True)
                for j in range(tn // LANES)]
        o_ref[b:b + 1, :] = jnp.concatenate(cols, axis=1) + b_ref[...]


def _adaln_mod(c, ada_w, ada_b):
    bsz, d = c.shape
    n = ada_w.shape[1]
    cb = jnp.broadcast_to(c[:, :, None], (bsz, d, LANES))
    return pl.pallas_call(
        _mod_kernel,
        grid=(n // MOD_TN,),
        in_specs=[
            pl.BlockSpec((bsz, d, LANES), lambda j: (0, 0, 0)),
            pl.BlockSpec((d, MOD_TN), lambda j: (0, j)),
            pl.BlockSpec((1, MOD_TN), lambda j: (0, j)),
        ],
        out_specs=pl.BlockSpec((bsz, MOD_TN), lambda j: (0, j)),
        out_shape=jax.ShapeDtypeStruct((bsz, n), F32),
        compiler_params=_params(),
        name="adaln_mod",
    )(cb, ada_w, ada_b.reshape(1, n))


def _rope_kernel(pos_ref, inv_ref, cos_ref, sin_ref):
    ang = pos_ref[...].astype(F32) * inv_ref[...]
    cos_ref[...] = jnp.cos(ang)
    sin_ref[...] = jnp.sin(ang)


def _rope_tables(positions):
    half = QK_ROPE // 2
    per_row = LANES // half
    t = positions.size
    inv_freq = ROPE_THETA ** (-jnp.arange(0, QK_ROPE, 2, dtype=F32) / QK_ROPE)
    pos = jnp.repeat(positions.reshape(t), half).reshape(t // per_row, LANES)
    inv = jnp.tile(inv_freq, per_row).reshape(1, LANES)
    rows = t // per_row
    tr = rows // 4
    cos, sin = pl.pallas_call(
        _rope_kernel,
        grid=(rows // tr,),
        in_specs=[pl.BlockSpec((tr, LANES), lambda i: (i, 0)),
                  pl.BlockSpec((1, LANES), lambda i: (0, 0))],
        out_specs=[pl.BlockSpec((tr, LANES), lambda i: (i, 0))] * 2,
        out_shape=[jax.ShapeDtypeStruct((rows, LANES), F32)] * 2,
        compiler_params=_params(),
        name="rope_tables",
    )(pos, inv)
    return cos.reshape(t, half), sin.reshape(t, half)


def _ffn_update(x, shift, scale, gate, norm_g, w1_ref, w3_ref, w2_ref):
    h = (x * _rms_scale(x) * (norm_g * (1.0 + scale)) + shift).astype(BF16)
    a = jnp.dot(h, w1_ref[...], preferred_element_type=F32)
    b = jnp.dot(h, w3_ref[...], preferred_element_type=F32)
    g = (a * jax.nn.sigmoid(a) * b).astype(BF16)
    o = jnp.dot(g, w2_ref[...], preferred_element_type=F32)
    return x + (0.5 * gate) * o


def _ffn1_kernel(x_ref, mod_ref, ng_ref, w1_ref, w3_ref, w2_ref, o_ref):
    mod = mod_ref[0]
    o_ref[...] = _ffn_update(x_ref[...], mod[0:1], mod[1:2], mod[2:3], ng_ref[...],
                             w1_ref, w3_ref, w2_ref)


def _ffn1(x2d, mod, norm_g, w1, w3, w2, tiles_per_batch):
    t, d = x2d.shape
    tm = ROW_TILE
    return pl.pallas_call(
        _ffn1_kernel,
        grid=(t // tm,),
        in_specs=[
            pl.BlockSpec((tm, d), lambda i: (i, 0)),
            pl.BlockSpec((1, N_MOD, d), lambda i: (i // tiles_per_batch, 0, 0)),
            _resident((1, d)),
            _resident(w1.shape), _resident(w3.shape), _resident(w2.shape),
        ],
        out_specs=pl.BlockSpec((tm, d), lambda i: (i, 0)),
        out_shape=jax.ShapeDtypeStruct((t, d), F32),
        compiler_params=_params(),
        name="ffn1",
    )(x2d, mod, norm_g, w1, w3, w2)


def _proj_kernel(x_ref, mod_ref, ng_ref, win_ref, cw_ref, gna_ref, qng_ref, wq_ref, kvng_ref,
                 wk_ref, wv_ref, cs_ref, cst_ref,
                 ya_ref, q_ref, k_ref, v_ref, ubuf, *, tiles_per_batch):
    i = pl.program_id(0)
    tm = x_ref.shape[0]
    cwid = CONV_WIDTH
    x = x_ref[...]
    mod = mod_ref[0]
    h = (x * _rms_scale(x) * (ng_ref[...] * (1.0 + mod[4:5])) + mod[3:4]).astype(BF16)
    z = jnp.dot(h, win_ref[...], preferred_element_type=F32)

    @pl.when(i % tiles_per_batch == 0)
    def _():
        ubuf[0:SUBLANES, :] = jnp.zeros((SUBLANES, cwid), F32)

    xb = z[:, 0:cwid]
    u = z[:, cwid:2 * cwid] * z[:, 2 * cwid:3 * cwid]
    ubuf[SUBLANES:SUBLANES + tm, :] = u
    u1 = ubuf[SUBLANES - 1:SUBLANES - 1 + tm, :]
    u2 = ubuf[SUBLANES - 2:SUBLANES - 2 + tm, :]
    cw = cw_ref[...]
    ya = xb * (cw[0:1] * u2 + cw[1:2] * u1 + cw[2:3] * u)
    ubuf[0:SUBLANES, :] = ubuf[tm:tm + SUBLANES, :]

    gw = cwid // CONV_GROUPS
    lane = lax.broadcasted_iota(jnp.int32, (tm, LANES), 1)
    low = lane < gw
    gna = gna_ref[...]
    outs = []
    for j in range(cwid // LANES):
        t_ = ya[:, j * LANES:(j + 1) * LANES]
        sq = t_ * t_
        s_lo = jnp.sum(jnp.where(low, sq, 0.0), axis=-1, keepdims=True)
        s_hi = jnp.sum(jnp.where(low, 0.0, sq), axis=-1, keepdims=True)
        ms = jnp.where(low, s_lo, s_hi) * (1.0 / gw)
        outs.append(t_ * lax.rsqrt(ms + EPS) * gna[:, j * LANES:(j + 1) * LANES])
    ya_ref[...] = jnp.concatenate(outs, axis=1).astype(ya_ref.dtype)

    c0 = 3 * cwid
    cq = z[:, c0:c0 + Q_LORA]
    ckv = z[:, c0 + Q_LORA:c0 + Q_LORA + KV_LORA]
    krr = z[:, c0 + Q_LORA + KV_LORA:c0 + Q_LORA + KV_LORA + LANES]
    cqn = (cq * _rms_scale(cq) * qng_ref[...]).astype(BF16)
    ckvn = (ckv * _rms_scale(ckv) * kvng_ref[...]).astype(BF16)

    nt = (((1,), (1,)), ((), ()))
    scale = (QK_NOPE + QK_ROPE) ** -0.5
    qt = lax.dot_general(wq_ref[...], cqn, nt, preferred_element_type=F32)
    cst = cst_ref[...]
    ct, st = cst[0:QK_ROPE], cst[QK_ROPE:2 * QK_ROPE]
    pieces = []
    for hh in range(MLA_HEADS):
        base = hh * QK_PAD
        nope = qt[base:base + QK_NOPE]
        rope = qt[base + QK_NOPE:base + QK_NOPE + QK_ROPE]
        rot = qt[base + QK_NOPE + QK_ROPE:base + QK_PAD]
        pieces += [nope * scale, (rope * ct + rot * st) * scale, jnp.zeros((QK_ROPE, tm), F32)]
    q_ref[0] = jnp.concatenate(pieces, axis=0).astype(q_ref.dtype)

    kn = jnp.dot(ckvn, wk_ref[...], preferred_element_type=F32)
    a = krr * cs_ref[...]
    kr = a + pltpu.roll(a, QK_ROPE, axis=1)
    kp = []
    for hh in range(MLA_HEADS):
        kp += [kn[:, hh * QK_NOPE:(hh + 1) * QK_NOPE], kr]
    k_ref[...] = jnp.concatenate(kp, axis=1).astype(k_ref.dtype)

    vt = lax.dot_general(wv_ref[...], ckvn, nt, preferred_element_type=F32)
    v_ref[0] = vt.astype(v_ref.dtype)


def _proj(x1, mod, norm_g, w_in_p, conv_w, gn_a, q_norm_g, wq_t, kv_norm_g, wk, wv_t, cs_row, cs_t,
          tiles_per_batch):
    t, d = x1.shape
    tm = ROW_TILE
    nt = t // tm
    hq = MLA_HEADS * QK_PAD
    return pl.pallas_call(
        functools.partial(_proj_kernel, tiles_per_batch=tiles_per_batch),
        grid=(nt,),
        in_specs=[
            pl.BlockSpec((tm, d), lambda i: (i, 0)),
            pl.BlockSpec((1, N_MOD, d), lambda i: (i // tiles_per_batch, 0, 0)),
            _resident((1, d)),
            _resident(w_in_p.shape),
            _resident(conv_w.shape),
            _resident(gn_a.shape),
            _resident(q_norm_g.shape),
            _resident(wq_t.shape),
            _resident(kv_norm_g.shape),
            _resident(wk.shape),
            _resident(wv_t.shape),
            pl.BlockSpec((tm, LANES), lambda i: (i, 0)),
            pl.BlockSpec((LANES, tm), lambda i: (0, i)),
        ],
        out_specs=[
            pl.BlockSpec((tm, CONV_WIDTH), lambda i: (i, 0)),
            pl.BlockSpec((1, hq, tm), lambda i: (i, 0, 0)),
            pl.BlockSpec((tm, hq), lambda i: (i, 0)),
            pl.BlockSpec((1, MLA_WIDTH, tm), lambda i: (i, 0, 0)),
        ],
        out_shape=[
            jax.ShapeDtypeStruct((t, CONV_WIDTH), BF16),
            jax.ShapeDtypeStruct((nt, hq, tm), BF16),
            jax.ShapeDtypeStruct((t, hq), BF16),
            jax.ShapeDtypeStruct((nt, MLA_WIDTH, tm), BF16),
        ],
        scratch_shapes=[pltpu.VMEM((tm + SUBLANES, CONV_WIDTH), F32)],
        compiler_params=_params(),
        name="mixer_proj",
    )(x1, mod, norm_g, w_in_p, conv_w, gn_a, q_norm_g, wq_t, kv_norm_g, wk, wv_t, cs_row, cs_t)


def _attn_kernel(q_ref, k_ref, v_ref, g_ref, o_ref, m_sc, l_sc, acc_sc):
    i = pl.program_id(1)
    tq = q_ref.shape[2]
    tk = v_ref.shape[2]
    m_sc[...] = jnp.full(m_sc.shape, -jnp.inf, F32)
    l_sc[...] = jnp.zeros(l_sc.shape, F32)
    acc_sc[...] = jnp.zeros(acc_sc.shape, F32)

    def step(j, masked):
        row0 = pl.multiple_of(j * tk, tk)
        for hh in range(MLA_HEADS):
            k_h = k_ref[pl.ds(row0, tk), hh * QK_PAD:(hh + 1) * QK_PAD]
            s = jnp.dot(k_h, q_ref[0, hh * QK_PAD:(hh + 1) * QK_PAD, :],
                        preferred_element_type=F32)
            if masked:
                kc = lax.broadcasted_iota(jnp.int32, (tk, tq), 0) // CHUNK
                qc = lax.broadcasted_iota(jnp.int32, (tk, tq), 1) // CHUNK
                s = jnp.where(kc <= qc, s, -1e30)
            m_prev = m_sc[hh]
            m_new = jnp.maximum(m_prev, jnp.max(s, axis=0, keepdims=True))
            alpha = jnp.exp(m_prev - m_new)
            p = jnp.exp(s - m_new)
            l_sc[hh] = alpha * l_sc[hh] + jnp.sum(p, axis=0, keepdims=True)
            pv = jnp.dot(v_ref[j, hh * V_HEAD:(hh + 1) * V_HEAD, :], p.astype(BF16),
                         preferred_element_type=F32)
            acc_sc[hh] = alpha * acc_sc[hh] + pv
            m_sc[hh] = m_new

    def full_step(j, carry):
        step(j, False)
        return carry

    lax.fori_loop(0, i, full_step, 0)
    step(i, True)

    g = g_ref[...]
    for hh in range(MLA_HEADS):
        o_t = acc_sc[hh] / l_sc[hh]
        o_t = o_t * lax.rsqrt(jnp.mean(o_t * o_t, axis=0, keepdims=True) + EPS)
        o_ref[:, hh * V_HEAD:(hh + 1) * V_HEAD] = (
            o_t.T * g[:, hh * V_HEAD:(hh + 1) * V_HEAD]).astype(o_ref.dtype)


def _attention(q_t, k, v_t, gn_b, bsz, seq):
    tq = ATTN_TQ
    tk = v_t.shape[2]
    assert tq == tk
    nq = seq // tq
    hq = MLA_HEADS * QK_PAD
    return pl.pallas_call(
        _attn_kernel,
        grid=(bsz, nq),
        in_specs=[
            pl.BlockSpec((1, hq, tq), lambda b, i: (b * nq + i, 0, 0)),
            pl.BlockSpec((seq, hq), lambda b, i: (b, 0), pipeline_mode=pl.Buffered(1)),
            pl.BlockSpec((seq // tk, MLA_WIDTH, tk), lambda b, i: (b, 0, 0),
                         pipeline_mode=pl.Buffered(1)),
            _resident(gn_b.shape),
        ],
        out_specs=pl.BlockSpec((tq, MLA_WIDTH), lambda b, i: (b * nq + i, 0)),
        out_shape=jax.ShapeDtypeStruct((bsz * seq, MLA_WIDTH), BF16),
        scratch_shapes=[
            pltpu.VMEM((MLA_HEADS, 1, tq), F32),
            pltpu.VMEM((MLA_HEADS, 1, tq), F32),
            pltpu.VMEM((MLA_HEADS, V_HEAD, tq), F32),
        ],
        compiler_params=_params(2),
        name="mla_attention",
    )(q_t, k, v_t, gn_b)


def _out_kernel(x_ref, ya_ref, yb_ref, mod_ref, wo_ref, ng_ref, w1_ref, w3_ref, w2_ref, fg_ref, o_ref):
    mod = mod_ref[0]
    y = jnp.concatenate([ya_ref[...], yb_ref[...]], axis=1)
    x = x_ref[...] + mod[5:6] * jnp.dot(y, wo_ref[...], preferred_element_type=F32)
    x = _ffn_update(x, mod[6:7], mod[7:8], mod[8:9], ng_ref[...], w1_ref, w3_ref, w2_ref)
    o_ref[...] = x * _rms_scale(x) * fg_ref[...]


def _out_ffn2(x1, ya, yb, mod, w_out, norm_g, w1, w3, w2, final_g, tiles_per_batch):
    t, d = x1.shape
    tm = ROW_TILE
    return pl.pallas_call(
        _out_kernel,
        grid=(t // tm,),
        in_specs=[
            pl.BlockSpec((tm, d), lambda i: (i, 0)),
            pl.BlockSpec((tm, CONV_WIDTH), lambda i: (i, 0)),
            pl.BlockSpec((tm, MLA_WIDTH), lambda i: (i, 0)),
            pl.BlockSpec((1, N_MOD, d), lambda i: (i // tiles_per_batch, 0, 0)),
            _resident(w_out.shape),
            _resident((1, d)),
            _resident(w1.shape), _resident(w3.shape), _resident(w2.shape),
            _resident((1, d)),
        ],
        out_specs=pl.BlockSpec((tm, d), lambda i: (i, 0)),
        out_shape=jax.ShapeDtypeStruct((t, d), F32),
        compiler_params=_params(),
        name="out_ffn2",
    )(x1, ya, yb, mod, w_out, norm_g, w1, w3, w2, final_g)


def _swap_halves(w, axis):
    lo, hi = jnp.split(w, 2, axis=axis)
    return jnp.concatenate([hi, lo], axis=axis)


def kernel(x, c, positions, ada_w, ada_b, norm_ffn1_g, ffn1_w1, ffn1_w3, ffn1_w2, norm_mix_g, w_in, conv_w, q_norm_g, w_uq, kv_norm_g, w_ukv, out_norm_g, w_out, norm_ffn2_g, ffn2_w1, ffn2_w3, ffn2_w2, final_norm_g):
    bsz, seq, d = x.shape
    depth = ada_w.shape[0]
    t = bsz * seq
    tiles_per_batch = seq // ROW_TILE
    half = QK_ROPE // 2

    cos, sin = _rope_tables(positions)
    cs_row = jnp.concatenate([cos, cos, -sin, sin], axis=1)
    cs_t = cs_row.T

    assert depth == 1, "only the one-layer block is implemented"
    l = 0
    xf = x.reshape(t, d)
    mod = _adaln_mod(c, ada_w[l], ada_b[l]).reshape(bsz, N_MOD, d)

    kr0 = 3 * CONV_WIDTH + Q_LORA + KV_LORA
    w_in_p = jnp.concatenate(
        [w_in[l], _swap_halves(w_in[l][:, kr0:kr0 + QK_ROPE], 1)], axis=1).astype(BF16)
    wq = w_uq[l].T.reshape(MLA_HEADS, QK_NOPE + QK_ROPE, Q_LORA)
    wq_rope = wq[:, QK_NOPE:]
    wq_t = jnp.concatenate([wq, _swap_halves(wq_rope, 1)], axis=1).reshape(
        MLA_HEADS * QK_PAD, Q_LORA).astype(BF16)
    wkv = w_ukv[l].reshape(KV_LORA, MLA_HEADS, QK_NOPE + V_HEAD)
    wk = wkv[:, :, :QK_NOPE].reshape(KV_LORA, MLA_HEADS * QK_NOPE).astype(BF16)
    wv_t = wkv[:, :, QK_NOPE:].reshape(KV_LORA, MLA_WIDTH).T.astype(BF16)

    x1 = _ffn1(xf, mod, norm_ffn1_g[l].reshape(1, d), ffn1_w1[l].astype(BF16),
               ffn1_w3[l].astype(BF16), ffn1_w2[l].astype(BF16), tiles_per_batch)
    ya, q_t, k, v_t = _proj(
        x1, mod, norm_mix_g[l].reshape(1, d), w_in_p, conv_w[l],
        out_norm_g[l, :CONV_WIDTH].reshape(1, CONV_WIDTH), q_norm_g[l].reshape(1, Q_LORA), wq_t,
        kv_norm_g[l].reshape(1, KV_LORA), wk, wv_t, cs_row, cs_t, tiles_per_batch)
    yb = _attention(q_t, k, v_t, out_norm_g[l, CONV_WIDTH:].reshape(1, MLA_WIDTH), bsz, seq)
    xf = _out_ffn2(x1, ya, yb, mod, w_out[l].astype(BF16), norm_ffn2_g[l].reshape(1, d),
                   ffn2_w1[l].astype(BF16), ffn2_w3[l].astype(BF16), ffn2_w2[l].astype(BF16),
                   final_norm_g.reshape(1, d), tiles_per_batch)
    return xf.reshape(bsz, seq, d)
```

```python
import functools
import math

import jax
import jax.numpy as jnp
from jax import lax
from jax.experimental import pallas as pl
from jax.experimental.pallas import tpu as pltpu

CHUNK = 64
EPS = 1e-6
N_MOD = 9
CONV_WIDTH = 512
CONV_GROUPS = 8
CONV_K = 3
MLA_HEADS = 4
QK_NOPE = 128
QK_ROPE = 64
V_HEAD = 128
Q_LORA = 384
KV_LORA = 256
ROPE_THETA = 10000.0
MLA_WIDTH = MLA_HEADS * V_HEAD

LANES = 128
SUBLANES = 8
QK_PAD = 256
BF16_ROWS = 16
V_ROWS = V_HEAD + BF16_ROWS
VMEM_LIMIT_BYTES = 56 * 1024 * 1024

ROW_TILE = 512
ATTN_TQ = 512
MOD_TN = 1024

F32 = jnp.float32
BF16 = jnp.bfloat16


def _params(n_axes=1):
    return pltpu.CompilerParams(
        dimension_semantics=("arbitrary",) * n_axes,
        vmem_limit_bytes=VMEM_LIMIT_BYTES,
    )


def _resident(shape):
    nd = len(shape)
    return pl.BlockSpec(shape, lambda *_: (0,) * nd, pipeline_mode=pl.Buffered(1))


def _rms_scale(x):
    return lax.rsqrt(jnp.mean(x * x, axis=-1, keepdims=True) + EPS)


def _mod_kernel(cb_ref, w_ref, b_ref, o_ref):
    w = w_ref[...]
    tn = w.shape[1]
    for b in range(cb_ref.shape[0]):
        cb = cb_ref[b]
        s = cb * jax.nn.sigmoid(cb)
        cols = [jnp.sum(w[:, j * LANES:(j + 1) * LANES] * s, axis=0, keepdims=True)
                for j in range(tn // LANES)]
        o_ref[b:b + 1, :] = jnp.concatenate(cols, axis=1) + b_ref[...]


def _adaln_mod(c, ada_w, ada_b):
    bsz, d = c.shape
    n = ada_w.shape[1]
    cb = jnp.broadcast_to(c[:, :, None], (bsz, d, LANES))
    return pl.pallas_call(
        _mod_kernel,
        grid=(n // MOD_TN,),
        in_specs=[
            pl.BlockSpec((bsz, d, LANES), lambda j: (0, 0, 0)),
            pl.BlockSpec((d, MOD_TN), lambda j: (0, j)),
            pl.BlockSpec((1, MOD_TN), lambda j: (0, j)),
        ],
        out_specs=pl.BlockSpec((bsz, MOD_TN), lambda j: (0, j)),
        out_shape=jax.ShapeDtypeStruct((bsz, n), F32),
        compiler_params=_params(),
        name="adaln_mod",
    )(cb, ada_w, ada_b.reshape(1, n))


def _rope_kernel(pos_ref, inv_ref, cos_ref, sin_ref):
    ang = pos_ref[...].astype(F32) * inv_ref[...]
    cos_ref[...] = jnp.cos(ang)
    sin_ref[...] = jnp.sin(ang)


def _rope_tables(positions):
    half = QK_ROPE // 2
    per_row = LANES // half
    t = positions.size
    inv_freq = ROPE_THETA ** (-jnp.arange(0, QK_ROPE, 2, dtype=F32) / QK_ROPE)
    pos = jnp.repeat(positions.reshape(t), half).reshape(t // per_row, LANES)
    inv = jnp.tile(inv_freq, per_row).reshape(1, LANES)
    rows = t // per_row
    tr = rows // 4
    cos, sin = pl.pallas_call(
        _rope_kernel,
        grid=(rows // tr,),
        in_specs=[pl.BlockSpec((tr, LANES), lambda i: (i, 0)),
                  pl.BlockSpec((1, LANES), lambda i: (0, 0))],
        out_specs=[pl.BlockSpec((tr, LANES), lambda i: (i, 0))] * 2,
        out_shape=[jax.ShapeDtypeStruct((rows, LANES), F32)] * 2,
        compiler_params=_params(),
        name="rope_tables",
    )(pos, inv)
    return cos.reshape(t, half), sin.reshape(t, half)


def _ffn_update(x, shift, scale, gate, norm_g, w1_ref, w3_ref, w2_ref):
    h = (x * _rms_scale(x) * (norm_g * (1.0 + scale)) + shift).astype(BF16)
    a = jnp.dot(h, w1_ref[...], preferred_element_type=F32)
    b = jnp.dot(h, w3_ref[...], preferred_element_type=F32)
    g = (a * jax.nn.sigmoid(a) * b).astype(BF16)
    o = jnp.dot(g, w2_ref[...], preferred_element_type=F32)
    return x + (0.5 * gate) * o


def _ffn1_kernel(x_ref, mod_ref, ng_ref, w1_ref, w3_ref, w2_ref, o_ref):
    mod = mod_ref[0]
    o_ref[...] = _ffn_update(x_ref[...], mod[0:1], mod[1:2], mod[2:3], ng_ref[...],
                             w1_ref, w3_ref, w2_ref)


def _ffn1(x2d, mod, norm_g, w1, w3, w2, tiles_per_batch):
    t, d = x2d.shape
    tm = ROW_TILE
    return pl.pallas_call(
        _ffn1_kernel,
        grid=(t // tm,),
        in_specs=[
            pl.BlockSpec((tm, d), lambda i: (i, 0)),
            pl.BlockSpec((1, N_MOD, d), lambda i: (i // tiles_per_batch, 0, 0)),
            _resident((1, d)),
            _resident(w1.shape), _resident(w3.shape), _resident(w2.shape),
        ],
        out_specs=pl.BlockSpec((tm, d), lambda i: (i, 0)),
        out_shape=jax.ShapeDtypeStruct((t, d), F32),
        compiler_params=_params(),
        name="ffn1",
    )(x2d, mod, norm_g, w1, w3, w2)


def _proj_kernel(x_ref, mod_ref, ng_ref, win_ref, cw_ref, gna_ref, qng_ref, wq_ref, kvng_ref,
                 wk_ref, wv_ref, cs_ref, cst_ref,
                 ya_ref, q_ref, k_ref, v_ref, ubuf, *, tiles_per_batch):
    i = pl.program_id(0)
    tm = x_ref.shape[0]
    cwid = CONV_WIDTH
    x = x_ref[...]
    mod = mod_ref[0]
    h = (x * _rms_scale(x) * (ng_ref[...] * (1.0 + mod[4:5])) + mod[3:4]).astype(BF16)
    z = jnp.dot(h, win_ref[...], preferred_element_type=F32)

    @pl.when(i % tiles_per_batch == 0)
    def _():
        ubuf[0:SUBLANES, :] = jnp.zeros((SUBLANES, cwid), F32)

    xb = z[:, 0:cwid]
    u = z[:, cwid:2 * cwid] * z[:, 2 * cwid:3 * cwid]
    ubuf[SUBLANES:SUBLANES + tm, :] = u
    u1 = ubuf[SUBLANES - 1:SUBLANES - 1 + tm, :]
    u2 = ubuf[SUBLANES - 2:SUBLANES - 2 + tm, :]
    cw = cw_ref[...]
    ya = xb * (cw[0:1] * u2 + cw[1:2] * u1 + cw[2:3] * u)
    ubuf[0:SUBLANES, :] = ubuf[tm:tm + SUBLANES, :]

    gw = cwid // CONV_GROUPS
    lane = lax.broadcasted_iota(jnp.int32, (tm, LANES), 1)
    low = lane < gw
    gna = gna_ref[...]
    outs = []
    for j in range(cwid // LANES):
        t_ = ya[:, j * LANES:(j + 1) * LANES]
        sq = t_ * t_
        s_lo = jnp.sum(jnp.where(low, sq, 0.0), axis=-1, keepdims=True)
        s_hi = jnp.sum(jnp.where(low, 0.0, sq), axis=-1, keepdims=True)
        ms = jnp.where(low, s_lo, s_hi) * (1.0 / gw)
        outs.append(t_ * lax.rsqrt(ms + EPS) * gna[:, j * LANES:(j + 1) * LANES])
    ya_ref[...] = jnp.concatenate(outs, axis=1).astype(ya_ref.dtype)

    c0 = 3 * cwid
    cq = z[:, c0:c0 + Q_LORA]
    ckv = z[:, c0 + Q_LORA:c0 + Q_LORA + KV_LORA]
    krr = z[:, c0 + Q_LORA + KV_LORA:c0 + Q_LORA + KV_LORA + LANES]
    cqn = (cq * _rms_scale(cq) * qng_ref[...]).astype(BF16)
    ckvn = (ckv * _rms_scale(ckv) * kvng_ref[...]).astype(BF16)

    nt = (((1,), (1,)), ((), ()))
    scale = (QK_NOPE + QK_ROPE) ** -0.5 * math.log2(math.e)
    qt = lax.dot_general(wq_ref[...], cqn, nt, preferred_element_type=F32)
    cst = cst_ref[...]
    ct, st = cst[0:QK_ROPE], cst[QK_ROPE:2 * QK_ROPE]
    pieces = []
    for hh in range(MLA_HEADS):
        base = hh * QK_PAD
        nope = qt[base:base + QK_NOPE]
        rope = qt[base + QK_NOPE:base + QK_NOPE + QK_ROPE]
        rot = qt[base + QK_NOPE + QK_ROPE:base + QK_PAD]
        pieces += [nope * scale, (rope * ct + rot * st) * scale, jnp.zeros((QK_ROPE, tm), F32)]
    q_ref[0] = jnp.concatenate(pieces, axis=0).astype(q_ref.dtype)

    kn = jnp.dot(ckvn, wk_ref[...], preferred_element_type=F32)
    a = krr * cs_ref[...]
    kr = a + pltpu.roll(a, QK_ROPE, axis=1)
    kp = []
    for hh in range(MLA_HEADS):
        kp += [kn[:, hh * QK_NOPE:(hh + 1) * QK_NOPE], kr]
    k_ref[...] = jnp.concatenate(kp, axis=1).astype(k_ref.dtype)

    vt = lax.dot_general(wv_ref[...], ckvn, nt, preferred_element_type=F32)
    ones = jnp.ones((V_ROWS - V_HEAD, tm), F32)
    vp = []
    for hh in range(MLA_HEADS):
        vp += [vt[hh * V_HEAD:(hh + 1) * V_HEAD], ones]
    v_ref[0] = jnp.concatenate(vp, axis=0).astype(v_ref.dtype)


def _proj(x1, mod, norm_g, w_in_p, conv_w, gn_a, q_norm_g, wq_t, kv_norm_g, wk, wv_t, cs_row, cs_t,
          tiles_per_batch):
    t, d = x1.shape
    tm = ROW_TILE
    nt = t // tm
    hq = MLA_HEADS * QK_PAD
    return pl.pallas_call(
        functools.partial(_proj_kernel, tiles_per_batch=tiles_per_batch),
        grid=(nt,),
        in_specs=[
            pl.BlockSpec((tm, d), lambda i: (i, 0)),
            pl.BlockSpec((1, N_MOD, d), lambda i: (i // tiles_per_batch, 0, 0)),
            _resident((1, d)),
            _resident(w_in_p.shape),
            _resident(conv_w.shape),
            _resident(gn_a.shape),
            _resident(q_norm_g.shape),
            _resident(wq_t.shape),
            _resident(kv_norm_g.shape),
            _resident(wk.shape),
            _resident(wv_t.shape),
            pl.BlockSpec((tm, LANES), lambda i: (i, 0)),
            pl.BlockSpec((LANES, tm), lambda i: (0, i)),
        ],
        out_specs=[
            pl.BlockSpec((tm, CONV_WIDTH), lambda i: (i, 0)),
            pl.BlockSpec((1, hq, tm), lambda i: (i, 0, 0)),
            pl.BlockSpec((tm, hq), lambda i: (i, 0)),
            pl.BlockSpec((1, MLA_HEADS * V_ROWS, tm), lambda i: (i, 0, 0)),
        ],
        out_shape=[
            jax.ShapeDtypeStruct((t, CONV_WIDTH), BF16),
            jax.ShapeDtypeStruct((nt, hq, tm), BF16),
            jax.ShapeDtypeStruct((t, hq), BF16),
            jax.ShapeDtypeStruct((nt, MLA_HEADS * V_ROWS, tm), BF16),
        ],
        scratch_shapes=[pltpu.VMEM((tm + SUBLANES, CONV_WIDTH), F32)],
        compiler_params=_params(),
        name="mixer_proj",
    )(x1, mod, norm_g, w_in_p, conv_w, gn_a, q_norm_g, wq_t, kv_norm_g, wk, wv_t, cs_row, cs_t)


def _attn_kernel(q_ref, k_ref, v_ref, g_ref, o_ref, s_sc, m_sc, acc_sc):
    i = pl.program_id(1)
    tq = q_ref.shape[2]
    tk = v_ref.shape[2]
    m_sc[...] = jnp.full(m_sc.shape, -jnp.inf, F32)
    acc_sc[...] = jnp.zeros(acc_sc.shape, F32)

    def scores(j, hh, slot):
        row0 = pl.multiple_of(j * tk, tk)
        s_sc[slot] = jnp.dot(k_ref[pl.ds(row0, tk), hh * QK_PAD:(hh + 1) * QK_PAD],
                             q_ref[0, hh * QK_PAD:(hh + 1) * QK_PAD, :],
                             preferred_element_type=F32)

    def softmax_pv(j, hh, slot, masked):
        s = s_sc[slot]
        if masked:
            kc = lax.broadcasted_iota(jnp.int32, (tk, tq), 0) // CHUNK
            qc = lax.broadcasted_iota(jnp.int32, (tk, tq), 1) // CHUNK
            s = jnp.where(kc <= qc, s, -1e30)
        m_prev = m_sc[hh]
        m_new = jnp.maximum(m_prev, jnp.max(s, axis=0, keepdims=True))
        alpha = jnp.exp2(m_prev - m_new)
        p = jnp.exp2(s - m_new).astype(BF16)
        pv = jnp.dot(v_ref[j, hh * V_ROWS:(hh + 1) * V_ROWS, :], p,
                     preferred_element_type=F32)
        acc_sc[hh] = alpha * acc_sc[hh] + pv
        m_sc[hh] = m_new

    scores(0, 0, 0)

    def full_block(j, carry):
        for hh in range(MLA_HEADS):
            if hh + 1 < MLA_HEADS:
                scores(j, hh + 1, (hh + 1) % 2)
            else:
                scores(j + 1, 0, 0)
            softmax_pv(j, hh, hh % 2, False)
        return carry

    lax.fori_loop(0, i, full_block, 0)
    for hh in range(MLA_HEADS):
        if hh + 1 < MLA_HEADS:
            scores(i, hh + 1, (hh + 1) % 2)
        softmax_pv(i, hh, hh % 2, True)

    g = g_ref[...]
    for hh in range(MLA_HEADS):
        acc = acc_sc[hh]
        o_t = acc[0:V_HEAD] / acc[V_HEAD:V_HEAD + 1]
        o_t = o_t * lax.rsqrt(jnp.mean(o_t * o_t, axis=0, keepdims=True) + EPS)
        o_ref[:, hh * V_HEAD:(hh + 1) * V_HEAD] = (
            o_t.T * g[:, hh * V_HEAD:(hh + 1) * V_HEAD]).astype(o_ref.dtype)


def _attention(q_t, k, v_t, gn_b, bsz, seq):
    tq = ATTN_TQ
    tk = v_t.shape[2]
    assert tq == tk
    nq = seq // tq
    hq = MLA_HEADS * QK_PAD
    return pl.pallas_call(
        _attn_kernel,
        grid=(bsz, nq),
        in_specs=[
            pl.BlockSpec((1, hq, tq), lambda b, i: (b * nq + i, 0, 0)),
            pl.BlockSpec((seq, hq), lambda b, i: (b, 0), pipeline_mode=pl.Buffered(1)),
            pl.BlockSpec((seq // tk, MLA_HEADS * V_ROWS, tk), lambda b, i: (b, 0, 0),
                         pipeline_mode=pl.Buffered(1)),
            _resident(gn_b.shape),
        ],
        out_specs=pl.BlockSpec((tq, MLA_WIDTH), lambda b, i: (b * nq + i, 0)),
        out_shape=jax.ShapeDtypeStruct((bsz * seq, MLA_WIDTH), BF16),
        scratch_shapes=[
            pltpu.VMEM((2, tk, tq), F32),
            pltpu.VMEM((MLA_HEADS, 1, tq), F32),
            pltpu.VMEM((MLA_HEADS, V_ROWS, tq), F32),
        ],
        compiler_params=_params(2),
        name="mla_attention",
    )(q_t, k, v_t, gn_b)


def _out_kernel(x_ref, ya_ref, yb_ref, mod_ref, wo_ref, ng_ref, w1_ref, w3_ref, w2_ref, fg_ref, o_ref):
    mod = mod_ref[0]
    y = jnp.concatenate([ya_ref[...], yb_ref[...]], axis=1)
    x = x_ref[...] + mod[5:6] * jnp.dot(y, wo_ref[...], preferred_element_type=F32)
    x = _ffn_update(x, mod[6:7], mod[7:8], mod[8:9], ng_ref[...], w1_ref, w3_ref, w2_ref)
    o_ref[...] = x * _rms_scale(x) * fg_ref[...]


def _out_ffn2(x1, ya, yb, mod, w_out, norm_g, w1, w3, w2, final_g, tiles_per_batch):
    t, d = x1.shape
    tm = ROW_TILE
    return pl.pallas_call(
        _out_kernel,
        grid=(t // tm,),
        in_specs=[
            pl.BlockSpec((tm, d), lambda i: (i, 0)),
            pl.BlockSpec((tm, CONV_WIDTH), lambda i: (i, 0)),
            pl.BlockSpec((tm, MLA_WIDTH), lambda i: (i, 0)),
            pl.BlockSpec((1, N_MOD, d), lambda i: (i // tiles_per_batch, 0, 0)),
            _resident(w_out.shape),
            _resident((1, d)),
            _resident(w1.shape), _resident(w3.shape), _resident(w2.shape),
            _resident((1, d)),
        ],
        out_specs=pl.BlockSpec((tm, d), lambda i: (i, 0)),
        out_shape=jax.ShapeDtypeStruct((t, d), F32),
        compiler_params=_params(),
        name="out_ffn2",
    )(x1, ya, yb, mod, w_out, norm_g, w1, w3, w2, final_g)


def _swap_halves(w, axis):
    lo, hi = jnp.split(w, 2, axis=axis)
    return jnp.concatenate([hi, lo], axis=axis)


def kernel(x, c, positions, ada_w, ada_b, norm_ffn1_g, ffn1_w1, ffn1_w3, ffn1_w2, norm_mix_g, w_in, conv_w, q_norm_g, w_uq, kv_norm_g, w_ukv, out_norm_g, w_out, norm_ffn2_g, ffn2_w1, ffn2_w3, ffn2_w2, final_norm_g):
    bsz, seq, d = x.shape
    depth = ada_w.shape[0]
    t = bsz * seq
    tiles_per_batch = seq // ROW_TILE
    half = QK_ROPE // 2

    cos, sin = _rope_tables(positions)
    cs_row = jnp.concatenate([cos, cos, -sin, sin], axis=1)
    cs_t = cs_row.T

    assert depth == 1, "only the one-layer block is implemented"
    l = 0
    xf = x.reshape(t, d)
    mod = _adaln_mod(c, ada_w[l], ada_b[l]).reshape(bsz, N_MOD, d)

    kr0 = 3 * CONV_WIDTH + Q_LORA + KV_LORA
    w_in_p = jnp.concatenate(
        [w_in[l], _swap_halves(w_in[l][:, kr0:kr0 + QK_ROPE], 1)], axis=1).astype(BF16)
    wq = w_uq[l].T.reshape(MLA_HEADS, QK_NOPE + QK_ROPE, Q_LORA)
    wq_rope = wq[:, QK_NOPE:]
    wq_t = jnp.concatenate([wq, _swap_halves(wq_rope, 1)], axis=1).reshape(
        MLA_HEADS * QK_PAD, Q_LORA).astype(BF16)
    wkv = w_ukv[l].reshape(KV_LORA, MLA_HEADS, QK_NOPE + V_HEAD)
    wk = wkv[:, :, :QK_NOPE].reshape(KV_LORA, MLA_HEADS * QK_NOPE).astype(BF16)
    wv_t = wkv[:, :, QK_NOPE:].reshape(KV_LORA, MLA_WIDTH).T.astype(BF16)

    x1 = _ffn1(xf, mod, norm_ffn1_g[l].reshape(1, d), ffn1_w1[l].astype(BF16),
               ffn1_w3[l].astype(BF16), ffn1_w2[l].astype(BF16), tiles_per_batch)
    ya, q_t, k, v_t = _proj(
        x1, mod, norm_mix_g[l].reshape(1, d), w_in_p, conv_w[l],
        out_norm_g[l, :CONV_WIDTH].reshape(1, CONV_WIDTH), q_norm_g[l].reshape(1, Q_LORA), wq_t,
        kv_norm_g[l].reshape(1, KV_LORA), wk, wv_t, cs_row, cs_t, tiles_per_batch)
    yb = _attention(q_t, k, v_t, out_norm_g[l, CONV_WIDTH:].reshape(1, MLA_WIDTH), bsz, seq)
    xf = _out_ffn2(x1, ya, yb, mod, w_out[l].astype(BF16), norm_ffn2_g[l].reshape(1, d),
                   ffn2_w1[l].astype(BF16), ffn2_w3[l].astype(BF16), ffn2_w2[l].astype(BF16),
                   final_norm_g.reshape(1, d), tiles_per_batch)
    return xf.reshape(bsz, seq, d)
```

```python
import functools
import math

import jax
import jax.numpy as jnp
from jax import lax
from jax.experimental import pallas as pl
from jax.experimental.pallas import tpu as pltpu

CHUNK = 64
EPS = 1e-6
N_MOD = 9
CONV_WIDTH = 512
CONV_GROUPS = 8
CONV_K = 3
MLA_HEADS = 4
QK_NOPE = 128
QK_ROPE = 64
V_HEAD = 128
Q_LORA = 384
KV_LORA = 256
ROPE_THETA = 10000.0
MLA_WIDTH = MLA_HEADS * V_HEAD

LANES = 128
SUBLANES = 8
MXU_TILE = 256
QK_PAD = MXU_TILE
BF16_ROWS = 16
V_ROWS = V_HEAD + BF16_ROWS
VMEM_LIMIT_BYTES = 56 * 1024 * 1024

ROW_TILE = 512
ATTN_TQ = 512
MOD_TN = 1024
ROPE_TC = 2048

F32 = jnp.float32
BF16 = jnp.bfloat16


def _params(n_axes=1):
    return pltpu.CompilerParams(
        dimension_semantics=("arbitrary",) * n_axes,
        vmem_limit_bytes=VMEM_LIMIT_BYTES,
    )


def _resident(shape):
    nd = len(shape)
    return pl.BlockSpec(shape, lambda *_: (0,) * nd, pipeline_mode=pl.Buffered(1))


def _rms_scale(x):
    return lax.rsqrt(jnp.mean(x * x, axis=-1, keepdims=True) + EPS)


def _mod_kernel(cb_ref, w_ref, b_ref, o_ref):
    w = w_ref[...]
    tn = w.shape[1]
    for b in range(cb_ref.shape[0]):
        cb = cb_ref[b]
        s = cb * jax.nn.sigmoid(cb)
        cols = [jnp.sum(w[:, j * LANES:(j + 1) * LANES] * s, axis=0, keepdims=True)
                for j in range(tn // LANES)]
        o_ref[b:b + 1, :] = jnp.concatenate(cols, axis=1) + b_ref[...]


def _adaln_mod(c, ada_w, ada_b):
    bsz, d = c.shape
    n = ada_w.shape[1]
    cb = jnp.broadcast_to(c[:, :, None], (bsz, d, LANES))
    return pl.pallas_call(
        _mod_kernel,
        grid=(n // MOD_TN,),
        in_specs=[
            pl.BlockSpec((bsz, d, LANES), lambda j: (0, 0, 0)),
            pl.BlockSpec((d, MOD_TN), lambda j: (0, j)),
            pl.BlockSpec((1, MOD_TN), lambda j: (0, j)),
        ],
        out_specs=pl.BlockSpec((bsz, MOD_TN), lambda j: (0, j)),
        out_shape=jax.ShapeDtypeStruct((bsz, n), F32),
        compiler_params=_params(),
        name="adaln_mod",
    )(cb, ada_w, ada_b.reshape(1, n))


def _rope_kernel(pos_ref, inv_ref, cst_ref, csr_ref):
    ang = pos_ref[...].astype(F32) * inv_ref[...]
    c = jnp.cos(ang)
    s = jnp.sin(ang)
    cst = jnp.concatenate([c, c, -s, s], axis=0)
    cst_ref[...] = cst
    csr_ref[...] = cst.T


def _rope_tables(positions):
    half = QK_ROPE // 2
    t = positions.size
    tc = ROPE_TC
    inv_freq = ROPE_THETA ** (-jnp.arange(0, QK_ROPE, 2, dtype=F32) / QK_ROPE)
    return pl.pallas_call(
        _rope_kernel,
        grid=(t // tc,),
        in_specs=[pl.BlockSpec((1, tc), lambda i: (0, i)),
                  pl.BlockSpec((half, 1), lambda i: (0, 0))],
        out_specs=[pl.BlockSpec((4 * half, tc), lambda i: (0, i)),
                   pl.BlockSpec((tc, 4 * half), lambda i: (i, 0))],
        out_shape=[jax.ShapeDtypeStruct((4 * half, t), F32),
                   jax.ShapeDtypeStruct((t, 4 * half), F32)],
        compiler_params=_params(),
        name="rope_tables",
    )(positions.reshape(1, t), inv_freq.reshape(half, 1))


def _ffn_update(x, shift, scale, gate, norm_g, w1_ref, w3_ref, w2_ref):
    h = (x * _rms_scale(x) * (norm_g * (1.0 + scale)) + shift).astype(BF16)
    a = jnp.dot(h, w1_ref[...], preferred_element_type=F32)
    b = jnp.dot(h, w3_ref[...], preferred_element_type=F32)
    g = (a * jax.nn.sigmoid(a) * b).astype(BF16)
    o = jnp.dot(g, w2_ref[...], preferred_element_type=F32)
    return x + (0.5 * gate) * o


def _ffn1_kernel(x_ref, mod_ref, ng_ref, w1_ref, w3_ref, w2_ref, o_ref):
    mod = mod_ref[0]
    o_ref[...] = _ffn_update(x_ref[...], mod[0:1], mod[1:2], mod[2:3], ng_ref[...],
                             w1_ref, w3_ref, w2_ref)


def _ffn1(x2d, mod, norm_g, w1, w3, w2, tiles_per_batch):
    t, d = x2d.shape
    tm = ROW_TILE
    return pl.pallas_call(
        _ffn1_kernel,
        grid=(t // tm,),
        in_specs=[
            pl.BlockSpec((tm, d), lambda i: (i, 0)),
            pl.BlockSpec((1, N_MOD, d), lambda i: (i // tiles_per_batch, 0, 0)),
            _resident((1, d)),
            _resident(w1.shape), _resident(w3.shape), _resident(w2.shape),
        ],
        out_specs=pl.BlockSpec((tm, d), lambda i: (i, 0)),
        out_shape=jax.ShapeDtypeStruct((t, d), F32),
        compiler_params=_params(),
        name="ffn1",
    )(x2d, mod, norm_g, w1, w3, w2)


def _proj_kernel(x_ref, mod_ref, ng_ref, win_ref, cw_ref, gna_ref, qng_ref, wq_ref, kvng_ref,
                 wk_ref, wv_ref, cs_ref, cst_ref,
                 ya_ref, q_ref, k_ref, v_ref, zbuf_a, zbuf_b, ubuf, *, tiles_per_batch):
    i = pl.program_id(0)

    @pl.when(i == 0)
    def _():
        zbuf_b[...] = jnp.zeros(zbuf_b.shape, F32)
        ubuf[0:SUBLANES, :] = jnp.zeros((SUBLANES, CONV_WIDTH), F32)

    def step(z_new, z_old):
        x = x_ref[...]
        mod = mod_ref[0]
        h = (x * _rms_scale(x) * (ng_ref[...] * (1.0 + mod[4:5])) + mod[3:4]).astype(BF16)
        post = _mixer_post_pieces(z_old, i - 1, cw_ref, gna_ref, qng_ref, wq_ref, kvng_ref, wk_ref,
                                  wv_ref, cs_ref, cst_ref, ya_ref, q_ref, k_ref, v_ref, ubuf,
                                  tiles_per_batch)
        n_chunks = win_ref.shape[1] // MXU_TILE
        assert len(post) == n_chunks
        for c in range(n_chunks):
            cols = slice(c * MXU_TILE, (c + 1) * MXU_TILE)
            z_new[:, cols] = jnp.dot(h, win_ref[:, cols], preferred_element_type=F32)
            post[c]()

    pl.when(i % 2 == 0)(functools.partial(step, zbuf_a, zbuf_b))
    pl.when(i % 2 == 1)(functools.partial(step, zbuf_b, zbuf_a))


def _mixer_post_pieces(z, tile, cw_ref, gna_ref, qng_ref, wq_ref, kvng_ref, wk_ref, wv_ref, cs_ref,
                       cst_ref, ya_ref, q_ref, k_ref, v_ref, ubuf, tiles_per_batch):
    tm = z.shape[0]
    cwid = CONV_WIDTH
    c0 = 3 * cwid
    nt = (((1,), (1,)), ((), ()))
    scale = (QK_NOPE + QK_ROPE) ** -0.5 * math.log2(math.e)
    state = {}

    def q_head(hh):
        if hh == 0:
            cq = z[:, c0:c0 + Q_LORA]
            state["cqn"] = (cq * _rms_scale(cq) * qng_ref[...]).astype(BF16)
        base = hh * QK_PAD
        qt = lax.dot_general(wq_ref[base:base + QK_PAD, :], state["cqn"], nt,
                             preferred_element_type=F32)
        ct = cst_ref[0:QK_ROPE, :]
        st = cst_ref[QK_ROPE:2 * QK_ROPE, :]
        rope = qt[QK_NOPE:QK_NOPE + QK_ROPE] * ct + qt[QK_NOPE + QK_ROPE:QK_PAD] * st
        q_ref[0, base:base + QK_PAD, :] = jnp.concatenate(
            [qt[0:QK_NOPE] * scale, rope * scale, jnp.zeros((QK_ROPE, tm), F32)],
            axis=0).astype(q_ref.dtype)

    def kv():
        ckv = z[:, c0 + Q_LORA:c0 + Q_LORA + KV_LORA]
        ckvn = (ckv * _rms_scale(ckv) * kvng_ref[...]).astype(BF16)
        kn = jnp.dot(ckvn, wk_ref[...], preferred_element_type=F32)
        krr = z[:, c0 + Q_LORA + KV_LORA:c0 + Q_LORA + KV_LORA + LANES]
        a = krr * cs_ref[...]
        kr = a + pltpu.roll(a, QK_ROPE, axis=1)
        kp = []
        for hh in range(MLA_HEADS):
            kp += [kn[:, hh * QK_NOPE:(hh + 1) * QK_NOPE], kr]
        k_ref[...] = jnp.concatenate(kp, axis=1).astype(k_ref.dtype)
        vt = lax.dot_general(wv_ref[...], ckvn, nt, preferred_element_type=F32)
        ones = jnp.ones((V_ROWS - V_HEAD, tm), F32)
        vp = []
        for hh in range(MLA_HEADS):
            vp += [vt[hh * V_HEAD:(hh + 1) * V_HEAD], ones]
        v_ref[0] = jnp.concatenate(vp, axis=0).astype(v_ref.dtype)

    def conv_tile(j):
        cols = slice(j * LANES, (j + 1) * LANES)
        ubuf[0:SUBLANES, cols] = jnp.where(tile % tiles_per_batch == 0, 0.0, ubuf[0:SUBLANES, cols])
        u = z[:, cwid + j * LANES:cwid + (j + 1) * LANES] * z[:, 2 * cwid + j * LANES:2 * cwid + (j + 1) * LANES]
        ubuf[SUBLANES:SUBLANES + tm, cols] = u
        u1 = ubuf[SUBLANES - 1:SUBLANES - 1 + tm, cols]
        u2 = ubuf[SUBLANES - 2:SUBLANES - 2 + tm, cols]
        cw = cw_ref[:, cols]
        ya = z[:, cols] * (cw[0:1] * u2 + cw[1:2] * u1 + cw[2:3] * u)
        ubuf[0:SUBLANES, cols] = ubuf[tm:tm + SUBLANES, cols]
        gw = cwid // CONV_GROUPS
        low = lax.broadcasted_iota(jnp.int32, (tm, LANES), 1) < gw
        sq = ya * ya
        s_lo = jnp.sum(jnp.where(low, sq, 0.0), axis=-1, keepdims=True)
        s_hi = jnp.sum(jnp.where(low, 0.0, sq), axis=-1, keepdims=True)
        ms = jnp.where(low, s_lo, s_hi) * (1.0 / gw)
        ya_ref[:, cols] = (ya * lax.rsqrt(ms + EPS) * gna_ref[:, cols]).astype(ya_ref.dtype)

    pieces = [functools.partial(q_head, hh) for hh in range(MLA_HEADS)] + [kv]
    pieces += [functools.partial(conv_tile, j) for j in range(cwid // LANES)]
    return pieces


def _proj(x1, mod, norm_g, w_in_p, conv_w, gn_a, q_norm_g, wq_t, kv_norm_g, wk, wv_t, cs_row, cs_t,
          tiles_per_batch):
    t, d = x1.shape
    tm = ROW_TILE
    nt = t // tm
    hq = MLA_HEADS * QK_PAD

    def cur(i):
        return jnp.minimum(i, nt - 1)

    def prev(i):
        return jnp.maximum(i - 1, 0)

    return pl.pallas_call(
        functools.partial(_proj_kernel, tiles_per_batch=tiles_per_batch),
        grid=(nt + 1,),
        in_specs=[
            pl.BlockSpec((tm, d), lambda i: (cur(i), 0)),
            pl.BlockSpec((1, N_MOD, d), lambda i: (cur(i) // tiles_per_batch, 0, 0)),
            _resident((1, d)),
            _resident(w_in_p.shape),
            _resident(conv_w.shape),
            _resident(gn_a.shape),
            _resident(q_norm_g.shape),
            _resident(wq_t.shape),
            _resident(kv_norm_g.shape),
            _resident(wk.shape),
            _resident(wv_t.shape),
            pl.BlockSpec((tm, LANES), lambda i: (prev(i), 0)),
            pl.BlockSpec((LANES, tm), lambda i: (0, prev(i))),
        ],
        out_specs=[
            pl.BlockSpec((tm, CONV_WIDTH), lambda i: (prev(i), 0)),
            pl.BlockSpec((1, hq, tm), lambda i: (prev(i), 0, 0)),
            pl.BlockSpec((tm, hq), lambda i: (prev(i), 0)),
            pl.BlockSpec((1, MLA_HEADS * V_ROWS, tm), lambda i: (prev(i), 0, 0)),
        ],
        out_shape=[
            jax.ShapeDtypeStruct((t, CONV_WIDTH), BF16),
            jax.ShapeDtypeStruct((nt, hq, tm), BF16),
            jax.ShapeDtypeStruct((t, hq), BF16),
            jax.ShapeDtypeStruct((nt, MLA_HEADS * V_ROWS, tm), BF16),
        ],
        scratch_shapes=[pltpu.VMEM((tm, w_in_p.shape[1]), F32),
                        pltpu.VMEM((tm, w_in_p.shape[1]), F32),
                        pltpu.VMEM((tm + SUBLANES, CONV_WIDTH), F32)],
        compiler_params=_params(),
        name="mixer_proj",
    )(x1, mod, norm_g, w_in_p, conv_w, gn_a, q_norm_g, wq_t, kv_norm_g, wk, wv_t, cs_row, cs_t)


def _attn_kernel(q_ref, k_ref, v_ref, g_ref, o_ref, s_sc, m_sc, acc_sc):
    i = pl.program_id(1)
    tq = q_ref.shape[2]
    tk = v_ref.shape[2]
    m_sc[...] = jnp.full(m_sc.shape, -jnp.inf, F32)
    acc_sc[...] = jnp.zeros(acc_sc.shape, F32)

    def scores(j, hh, slot):
        row0 = pl.multiple_of(j * tk, tk)
        s_sc[slot] = jnp.dot(k_ref[pl.ds(row0, tk), hh * QK_PAD:(hh + 1) * QK_PAD],
                             q_ref[0, hh * QK_PAD:(hh + 1) * QK_PAD, :],
                             preferred_element_type=F32)

    def softmax_pv(j, hh, slot, masked):
        s = s_sc[slot]
        if masked:
            kc = lax.broadcasted_iota(jnp.int32, (tk, tq), 0) // CHUNK
            qc = lax.broadcasted_iota(jnp.int32, (tk, tq), 1) // CHUNK
            s = jnp.where(kc <= qc, s, -1e30)
        m_prev = m_sc[hh]
        m_new = jnp.maximum(m_prev, jnp.max(s, axis=0, keepdims=True))
        alpha = jnp.exp2(m_prev - m_new)
        p = jnp.exp2(s - m_new).astype(BF16)
        pv = jnp.dot(v_ref[j, hh * V_ROWS:(hh + 1) * V_ROWS, :], p,
                     preferred_element_type=F32)
        acc_sc[hh] = alpha * acc_sc[hh] + pv
        m_sc[hh] = m_new

    scores(0, 0, 0)

    def full_block(j, carry):
        for hh in range(MLA_HEADS):
            if hh + 1 < MLA_HEADS:
                scores(j, hh + 1, (hh + 1) % 2)
            else:
                scores(j + 1, 0, 0)
            softmax_pv(j, hh, hh % 2, False)
        return carry

    lax.fori_loop(0, i, full_block, 0)
    for hh in range(MLA_HEADS):
        if hh + 1 < MLA_HEADS:
            scores(i, hh + 1, (hh + 1) % 2)
        softmax_pv(i, hh, hh % 2, True)

    g = g_ref[...]
    for hh in range(MLA_HEADS):
        acc = acc_sc[hh]
        o_t = acc[0:V_HEAD] / acc[V_HEAD:V_HEAD + 1]
        o_t = o_t * lax.rsqrt(jnp.mean(o_t * o_t, axis=0, keepdims=True) + EPS)
        o_ref[:, hh * V_HEAD:(hh + 1) * V_HEAD] = (
            o_t.T * g[:, hh * V_HEAD:(hh + 1) * V_HEAD]).astype(o_ref.dtype)


def _attention(q_t, k, v_t, gn_b, bsz, seq):
    tq = ATTN_TQ
    tk = v_t.shape[2]
    assert tq == tk
    nq = seq // tq
    hq = MLA_HEADS * QK_PAD
    return pl.pallas_call(
        _attn_kernel,
        grid=(bsz, nq),
        in_specs=[
            pl.BlockSpec((1, hq, tq), lambda b, i: (b * nq + i, 0, 0)),
            pl.BlockSpec((seq, hq), lambda b, i: (b, 0), pipeline_mode=pl.Buffered(1)),
            pl.BlockSpec((seq // tk, MLA_HEADS * V_ROWS, tk), lambda b, i: (b, 0, 0),
                         pipeline_mode=pl.Buffered(1)),
            _resident(gn_b.shape),
        ],
        out_specs=pl.BlockSpec((tq, MLA_WIDTH), lambda b, i: (b * nq + i, 0)),
        out_shape=jax.ShapeDtypeStruct((bsz * seq, MLA_WIDTH), BF16),
        scratch_shapes=[
            pltpu.VMEM((2, tk, tq), F32),
            pltpu.VMEM((MLA_HEADS, 1, tq), F32),
            pltpu.VMEM((MLA_HEADS, V_ROWS, tq), F32),
        ],
        compiler_params=_params(2),
        name="mla_attention",
    )(q_t, k, v_t, gn_b)


def _out_kernel(x_ref, ya_ref, yb_ref, mod_ref, wo_ref, ng_ref, w1_ref, w3_ref, w2_ref, fg_ref, o_ref):
    mod = mod_ref[0]
    y = jnp.concatenate([ya_ref[...], yb_ref[...]], axis=1)
    x = x_ref[...] + mod[5:6] * jnp.dot(y, wo_ref[...], preferred_element_type=F32)
    x = _ffn_update(x, mod[6:7], mod[7:8], mod[8:9], ng_ref[...], w1_ref, w3_ref, w2_ref)
    o_ref[...] = x * _rms_scale(x) * fg_ref[...]


def _out_ffn2(x1, ya, yb, mod, w_out, norm_g, w1, w3, w2, final_g, tiles_per_batch):
    t, d = x1.shape
    tm = ROW_TILE
    return pl.pallas_call(
        _out_kernel,
        grid=(t // tm,),
        in_specs=[
            pl.BlockSpec((tm, d), lambda i: (i, 0)),
            pl.BlockSpec((tm, CONV_WIDTH), lambda i: (i, 0)),
            pl.BlockSpec((tm, MLA_WIDTH), lambda i: (i, 0)),
            pl.BlockSpec((1, N_MOD, d), lambda i: (i // tiles_per_batch, 0, 0)),
            _resident(w_out.shape),
            _resident((1, d)),
            _resident(w1.shape), _resident(w3.shape), _resident(w2.shape),
            _resident((1, d)),
        ],
        out_specs=pl.BlockSpec((tm, d), lambda i: (i, 0)),
        out_shape=jax.ShapeDtypeStruct((t, d), F32),
        compiler_params=_params(),
        name="out_ffn2",
    )(x1, ya, yb, mod, w_out, norm_g, w1, w3, w2, final_g)


def _swap_halves(w, axis):
    lo, hi = jnp.split(w, 2, axis=axis)
    return jnp.concatenate([hi, lo], axis=axis)


def kernel(x, c, positions, ada_w, ada_b, norm_ffn1_g, ffn1_w1, ffn1_w3, ffn1_w2, norm_mix_g, w_in, conv_w, q_norm_g, w_uq, kv_norm_g, w_ukv, out_norm_g, w_out, norm_ffn2_g, ffn2_w1, ffn2_w3, ffn2_w2, final_norm_g):
    bsz, seq, d = x.shape
    depth = ada_w.shape[0]
    t = bsz * seq
    tiles_per_batch = seq // ROW_TILE
    half = QK_ROPE // 2

    cs_t, cs_row = _rope_tables(positions)

    assert depth == 1, "only the one-layer block is implemented"
    l = 0
    xf = x.reshape(t, d)
    mod = _adaln_mod(c, ada_w[l], ada_b[l]).reshape(bsz, N_MOD, d)

    kr0 = 3 * CONV_WIDTH + Q_LORA + KV_LORA
    w_in_p = jnp.concatenate(
        [w_in[l], _swap_halves(w_in[l][:, kr0:kr0 + QK_ROPE], 1)], axis=1).astype(BF16)
    wq = w_uq[l].T.reshape(MLA_HEADS, QK_NOPE + QK_ROPE, Q_LORA)
    wq_rope = wq[:, QK_NOPE:]
    wq_t = jnp.concatenate([wq, _swap_halves(wq_rope, 1)], axis=1).reshape(
        MLA_HEADS * QK_PAD, Q_LORA).astype(BF16)
    wkv = w_ukv[l].reshape(KV_LORA, MLA_HEADS, QK_NOPE + V_HEAD)
    wk = wkv[:, :, :QK_NOPE].reshape(KV_LORA, MLA_HEADS * QK_NOPE).astype(BF16)
    wv_t = wkv[:, :, QK_NOPE:].reshape(KV_LORA, MLA_WIDTH).T.astype(BF16)

    x1 = _ffn1(xf, mod, norm_ffn1_g[l].reshape(1, d), ffn1_w1[l].astype(BF16),
               ffn1_w3[l].astype(BF16), ffn1_w2[l].astype(BF16), tiles_per_batch)
    ya, q_t, k, v_t = _proj(
        x1, mod, norm_mix_g[l].reshape(1, d), w_in_p, conv_w[l],
        out_norm_g[l, :CONV_WIDTH].reshape(1, CONV_WIDTH), q_norm_g[l].reshape(1, Q_LORA), wq_t,
        kv_norm_g[l].reshape(1, KV_LORA), wk, wv_t, cs_row, cs_t, tiles_per_batch)
    yb = _attention(q_t, k, v_t, out_norm_g[l, CONV_WIDTH:].reshape(1, MLA_WIDTH), bsz, seq)
    xf = _out_ffn2(x1, ya, yb, mod, w_out[l].astype(BF16), norm_ffn2_g[l].reshape(1, d),
                   ffn2_w1[l].astype(BF16), ffn2_w3[l].astype(BF16), ffn2_w2[l].astype(BF16),
                   final_norm_g.reshape(1, d), tiles_per_batch)
    return xf.reshape(bsz, seq, d)
```

```python
import functools
import math

import jax
import jax.numpy as jnp
from jax import lax
from jax.experimental import pallas as pl
from jax.experimental.pallas import tpu as pltpu

CHUNK = 64
EPS = 1e-6
N_MOD = 9
CONV_WIDTH = 512
CONV_GROUPS = 8
CONV_K = 3
MLA_HEADS = 4
QK_NOPE = 128
QK_ROPE = 64
V_HEAD = 128
Q_LORA = 384
KV_LORA = 256
ROPE_THETA = 10000.0
MLA_WIDTH = MLA_HEADS * V_HEAD

LANES = 128
SUBLANES = 8
MXU_TILE = 256
QK_PAD = MXU_TILE
BF16_ROWS = 16
V_ROWS = V_HEAD + BF16_ROWS
VMEM_LIMIT_BYTES = 56 * 1024 * 1024

ROW_TILE = 512
ATTN_TQ = 512
ATTN_UNROLL = 4
MOD_TN = 1024
ROPE_TC = 2048

F32 = jnp.float32
BF16 = jnp.bfloat16


def _params(n_axes=1, flags=None):
    return pltpu.CompilerParams(
        dimension_semantics=("arbitrary",) * n_axes,
        vmem_limit_bytes=VMEM_LIMIT_BYTES,
        flags=flags,
    )


def _resident(shape):
    nd = len(shape)
    return pl.BlockSpec(shape, lambda *_: (0,) * nd, pipeline_mode=pl.Buffered(1))


def _rms_scale(x):
    return lax.rsqrt(jnp.mean(x * x, axis=-1, keepdims=True) + EPS)


def _mod_kernel(cb_ref, w_ref, b_ref, o_ref):
    w = w_ref[...]
    tn = w.shape[1]
    for b in range(cb_ref.shape[0]):
        cb = cb_ref[b]
        s = cb * jax.nn.sigmoid(cb)
        cols = [jnp.sum(w[:, j * LANES:(j + 1) * LANES] * s, axis=0, keepdims=True)
                for j in range(tn // LANES)]
        o_ref[b:b + 1, :] = jnp.concatenate(cols, axis=1) + b_ref[...]


def _adaln_mod(c, ada_w, ada_b):
    bsz, d = c.shape
    n = ada_w.shape[1]
    cb = jnp.broadcast_to(c[:, :, None], (bsz, d, LANES))
    return pl.pallas_call(
        _mod_kernel,
        grid=(n // MOD_TN,),
        in_specs=[
            pl.BlockSpec((bsz, d, LANES), lambda j: (0, 0, 0)),
            pl.BlockSpec((d, MOD_TN), lambda j: (0, j)),
            pl.BlockSpec((1, MOD_TN), lambda j: (0, j)),
        ],
        out_specs=pl.BlockSpec((bsz, MOD_TN), lambda j: (0, j)),
        out_shape=jax.ShapeDtypeStruct((bsz, n), F32),
        compiler_params=_params(),
        name="adaln_mod",
    )(cb, ada_w, ada_b.reshape(1, n))


def _rope_kernel(pos_ref, inv_ref, cst_ref, csr_ref):
    ang = pos_ref[...].astype(F32) * inv_ref[...]
    c = jnp.cos(ang)
    s = jnp.sin(ang)
    cst = jnp.concatenate([c, c, -s, s], axis=0)
    cst_ref[...] = cst
    csr_ref[...] = cst.T


def _rope_tables(positions):
    half = QK_ROPE // 2
    t = positions.size
    tc = ROPE_TC
    inv_freq = ROPE_THETA ** (-jnp.arange(0, QK_ROPE, 2, dtype=F32) / QK_ROPE)
    return pl.pallas_call(
        _rope_kernel,
        grid=(t // tc,),
        in_specs=[pl.BlockSpec((1, tc), lambda i: (0, i)),
                  pl.BlockSpec((half, 1), lambda i: (0, 0))],
        out_specs=[pl.BlockSpec((4 * half, tc), lambda i: (0, i)),
                   pl.BlockSpec((tc, 4 * half), lambda i: (i, 0))],
        out_shape=[jax.ShapeDtypeStruct((4 * half, t), F32),
                   jax.ShapeDtypeStruct((t, 4 * half), F32)],
        compiler_params=_params(),
        name="rope_tables",
    )(positions.reshape(1, t), inv_freq.reshape(half, 1))


def _ffn_update(x, shift, scale, gate, norm_g, w1_ref, w3_ref, w2_ref):
    h = (x * _rms_scale(x) * (norm_g * (1.0 + scale)) + shift).astype(BF16)
    a = jnp.dot(h, w1_ref[...], preferred_element_type=F32)
    b = jnp.dot(h, w3_ref[...], preferred_element_type=F32)
    g = (a * jax.nn.sigmoid(a) * b).astype(BF16)
    o = jnp.dot(g, w2_ref[...], preferred_element_type=F32)
    return x + (0.5 * gate) * o


def _ffn1_kernel(x_ref, mod_ref, ng_ref, w1_ref, w3_ref, w2_ref, o_ref):
    mod = mod_ref[0]
    o_ref[...] = _ffn_update(x_ref[...], mod[0:1], mod[1:2], mod[2:3], ng_ref[...],
                             w1_ref, w3_ref, w2_ref)


def _ffn1(x2d, mod, norm_g, w1, w3, w2, tiles_per_batch):
    t, d = x2d.shape
    tm = ROW_TILE
    return pl.pallas_call(
        _ffn1_kernel,
        grid=(t // tm,),
        in_specs=[
            pl.BlockSpec((tm, d), lambda i: (i, 0)),
            pl.BlockSpec((1, N_MOD, d), lambda i: (i // tiles_per_batch, 0, 0)),
            _resident((1, d)),
            _resident(w1.shape), _resident(w3.shape), _resident(w2.shape),
        ],
        out_specs=pl.BlockSpec((tm, d), lambda i: (i, 0)),
        out_shape=jax.ShapeDtypeStruct((t, d), F32),
        compiler_params=_params(),
        name="ffn1",
    )(x2d, mod, norm_g, w1, w3, w2)


def _proj_kernel(x_ref, mod_ref, ng_ref, win_ref, cw_ref, gna_ref, qng_ref, wq_ref, kvng_ref,
                 wk_ref, wv_ref, cs_ref, cst_ref,
                 ya_ref, q_ref, k_ref, v_ref, zbuf_a, zbuf_b, ubuf, *, tiles_per_batch):
    i = pl.program_id(0)

    @pl.when(i == 0)
    def _():
        zbuf_b[...] = jnp.zeros(zbuf_b.shape, F32)
        ubuf[0:SUBLANES, :] = jnp.zeros((SUBLANES, CONV_WIDTH), F32)

    def step(z_new, z_old):
        x = x_ref[...]
        mod = mod_ref[0]
        h = (x * _rms_scale(x) * (ng_ref[...] * (1.0 + mod[4:5])) + mod[3:4]).astype(BF16)
        post = _mixer_post_pieces(z_old, i - 1, cw_ref, gna_ref, qng_ref, wq_ref, kvng_ref, wk_ref,
                                  wv_ref, cs_ref, cst_ref, ya_ref, q_ref, k_ref, v_ref, ubuf,
                                  tiles_per_batch)
        n_chunks = win_ref.shape[1] // MXU_TILE
        assert len(post) == n_chunks
        for c in range(n_chunks):
            cols = slice(c * MXU_TILE, (c + 1) * MXU_TILE)
            z_new[:, cols] = jnp.dot(h, win_ref[:, cols], preferred_element_type=F32)
            post[c]()

    pl.when(i % 2 == 0)(functools.partial(step, zbuf_a, zbuf_b))
    pl.when(i % 2 == 1)(functools.partial(step, zbuf_b, zbuf_a))


def _mixer_post_pieces(z, tile, cw_ref, gna_ref, qng_ref, wq_ref, kvng_ref, wk_ref, wv_ref, cs_ref,
                       cst_ref, ya_ref, q_ref, k_ref, v_ref, ubuf, tiles_per_batch):
    tm = z.shape[0]
    cwid = CONV_WIDTH
    c0 = 3 * cwid
    nt = (((1,), (1,)), ((), ()))
    scale = (QK_NOPE + QK_ROPE) ** -0.5 * math.log2(math.e)
    state = {}

    def q_head(hh):
        if hh == 0:
            cq = z[:, c0:c0 + Q_LORA]
            state["cqn"] = (cq * _rms_scale(cq) * qng_ref[...]).astype(BF16)
        base = hh * QK_PAD
        qt = lax.dot_general(wq_ref[base:base + QK_PAD, :], state["cqn"], nt,
                             preferred_element_type=F32)
        ct = cst_ref[0:QK_ROPE, :]
        st = cst_ref[QK_ROPE:2 * QK_ROPE, :]
        rope = qt[QK_NOPE:QK_NOPE + QK_ROPE] * ct + qt[QK_NOPE + QK_ROPE:QK_PAD] * st
        q_ref[0, base:base + QK_PAD, :] = jnp.concatenate(
            [qt[0:QK_NOPE] * scale, rope * scale, jnp.zeros((QK_ROPE, tm), F32)],
            axis=0).astype(q_ref.dtype)

    def kv():
        ckv = z[:, c0 + Q_LORA:c0 + Q_LORA + KV_LORA]
        ckvn = (ckv * _rms_scale(ckv) * kvng_ref[...]).astype(BF16)
        kn = jnp.dot(ckvn, wk_ref[...], preferred_element_type=F32)
        krr = z[:, c0 + Q_LORA + KV_LORA:c0 + Q_LORA + KV_LORA + LANES]
        a = krr * cs_ref[...]
        kr = a + pltpu.roll(a, QK_ROPE, axis=1)
        kp = []
        for hh in range(MLA_HEADS):
            kp += [kn[:, hh * QK_NOPE:(hh + 1) * QK_NOPE], kr]
        k_ref[...] = jnp.concatenate(kp, axis=1).astype(k_ref.dtype)
        vt = lax.dot_general(wv_ref[...], ckvn, nt, preferred_element_type=F32)
        ones = jnp.ones((V_ROWS - V_HEAD, tm), F32)
        vp = []
        for hh in range(MLA_HEADS):
            vp += [vt[hh * V_HEAD:(hh + 1) * V_HEAD], ones]
        v_ref[0] = jnp.concatenate(vp, axis=0).astype(v_ref.dtype)

    def conv_tile(j):
        cols = slice(j * LANES, (j + 1) * LANES)
        ubuf[0:SUBLANES, cols] = jnp.where(tile % tiles_per_batch == 0, 0.0, ubuf[0:SUBLANES, cols])
        u = z[:, cwid + j * LANES:cwid + (j + 1) * LANES] * z[:, 2 * cwid + j * LANES:2 * cwid + (j + 1) * LANES]
        ubuf[SUBLANES:SUBLANES + tm, cols] = u
        u1 = ubuf[SUBLANES - 1:SUBLANES - 1 + tm, cols]
        u2 = ubuf[SUBLANES - 2:SUBLANES - 2 + tm, cols]
        cw = cw_ref[:, cols]
        ya = z[:, cols] * (cw[0:1] * u2 + cw[1:2] * u1 + cw[2:3] * u)
        ubuf[0:SUBLANES, cols] = ubuf[tm:tm + SUBLANES, cols]
        gw = cwid // CONV_GROUPS
        low = lax.broadcasted_iota(jnp.int32, (tm, LANES), 1) < gw
        sq = ya * ya
        s_lo = jnp.sum(jnp.where(low, sq, 0.0), axis=-1, keepdims=True)
        s_hi = jnp.sum(jnp.where(low, 0.0, sq), axis=-1, keepdims=True)
        ms = jnp.where(low, s_lo, s_hi) * (1.0 / gw)
        ya_ref[:, cols] = (ya * lax.rsqrt(ms + EPS) * gna_ref[:, cols]).astype(ya_ref.dtype)

    pieces = [functools.partial(q_head, hh) for hh in range(MLA_HEADS)] + [kv]
    pieces += [functools.partial(conv_tile, j) for j in range(cwid // LANES)]
    return pieces


def _proj(x1, mod, norm_g, w_in_p, conv_w, gn_a, q_norm_g, wq_t, kv_norm_g, wk, wv_t, cs_row, cs_t,
          tiles_per_batch):
    t, d = x1.shape
    tm = ROW_TILE
    nt = t // tm
    hq = MLA_HEADS * QK_PAD

    def cur(i):
        return jnp.minimum(i, nt - 1)

    def prev(i):
        return jnp.maximum(i - 1, 0)

    return pl.pallas_call(
        functools.partial(_proj_kernel, tiles_per_batch=tiles_per_batch),
        grid=(nt + 1,),
        in_specs=[
            pl.BlockSpec((tm, d), lambda i: (cur(i), 0)),
            pl.BlockSpec((1, N_MOD, d), lambda i: (cur(i) // tiles_per_batch, 0, 0)),
            _resident((1, d)),
            _resident(w_in_p.shape),
            _resident(conv_w.shape),
            _resident(gn_a.shape),
            _resident(q_norm_g.shape),
            _resident(wq_t.shape),
            _resident(kv_norm_g.shape),
            _resident(wk.shape),
            _resident(wv_t.shape),
            pl.BlockSpec((tm, LANES), lambda i: (prev(i), 0)),
            pl.BlockSpec((LANES, tm), lambda i: (0, prev(i))),
        ],
        out_specs=[
            pl.BlockSpec((tm, CONV_WIDTH), lambda i: (prev(i), 0)),
            pl.BlockSpec((1, hq, tm), lambda i: (prev(i), 0, 0)),
            pl.BlockSpec((tm, hq), lambda i: (prev(i), 0)),
            pl.BlockSpec((1, MLA_HEADS * V_ROWS, tm), lambda i: (prev(i), 0, 0)),
        ],
        out_shape=[
            jax.ShapeDtypeStruct((t, CONV_WIDTH), BF16),
            jax.ShapeDtypeStruct((nt, hq, tm), BF16),
            jax.ShapeDtypeStruct((t, hq), BF16),
            jax.ShapeDtypeStruct((nt, MLA_HEADS * V_ROWS, tm), BF16),
        ],
        scratch_shapes=[pltpu.VMEM((tm, w_in_p.shape[1]), F32),
                        pltpu.VMEM((tm, w_in_p.shape[1]), F32),
                        pltpu.VMEM((tm + SUBLANES, CONV_WIDTH), F32)],
        compiler_params=_params(),
        name="mixer_proj",
    )(x1, mod, norm_g, w_in_p, conv_w, gn_a, q_norm_g, wq_t, kv_norm_g, wk, wv_t, cs_row, cs_t)


def _attn_kernel(q_ref, k_ref, v_ref, g_ref, o_ref, s_sc, m_sc, acc_sc):
    i = pl.program_id(1)
    tq = q_ref.shape[2]
    tk = v_ref.shape[2]
    m_sc[...] = jnp.full(m_sc.shape, -jnp.inf, F32)
    acc_sc[...] = jnp.zeros(acc_sc.shape, F32)

    def scores(j, hh, slot):
        row0 = pl.multiple_of(j * tk, tk)
        s_sc[slot] = jnp.dot(k_ref[pl.ds(row0, tk), hh * QK_PAD:(hh + 1) * QK_PAD],
                             q_ref[0, hh * QK_PAD:(hh + 1) * QK_PAD, :],
                             preferred_element_type=F32)

    def softmax_pv(j, hh, slot, masked):
        s = s_sc[slot]
        if masked:
            kc = lax.broadcasted_iota(jnp.int32, (tk, tq), 0) // CHUNK
            qc = lax.broadcasted_iota(jnp.int32, (tk, tq), 1) // CHUNK
            s = jnp.where(kc <= qc, s, -1e30)
        m_prev = m_sc[hh]
        m_new = jnp.maximum(m_prev, jnp.max(s, axis=0, keepdims=True))
        alpha = jnp.exp2(m_prev - m_new)
        p = jnp.exp2(s - m_new).astype(BF16)
        pv = jnp.dot(v_ref[j, hh * V_ROWS:(hh + 1) * V_ROWS, :], p,
                     preferred_element_type=F32)
        acc_sc[hh] = alpha * acc_sc[hh] + pv
        m_sc[hh] = m_new

    scores(0, 0, 0)

    def block(j, masked, next_j):
        for hh in range(MLA_HEADS):
            if hh + 1 < MLA_HEADS:
                scores(j, hh + 1, (hh + 1) % 2)
            elif next_j is not None:
                scores(next_j, 0, 0)
            softmax_pv(j, hh, hh % 2, masked)

    def run(j0, n):
        for t in range(n):
            block(j0 + t, False, j0 + t + 1)

    n_quads = lax.shift_right_logical(i, 2)

    def quad(q, carry):
        run(q * ATTN_UNROLL, ATTN_UNROLL)
        return carry

    lax.fori_loop(0, n_quads, quad, 0)
    done = n_quads * ATTN_UNROLL

    @pl.when((i & 2) != 0)
    def _():
        run(done, 2)

    @pl.when((i & 1) != 0)
    def _():
        run(done + (i & 2), 1)

    block(i, True, None)

    g = g_ref[...]
    for hh in range(MLA_HEADS):
        acc = acc_sc[hh]
        o_t = acc[0:V_HEAD] / acc[V_HEAD:V_HEAD + 1]
        o_t = o_t * lax.rsqrt(jnp.mean(o_t * o_t, axis=0, keepdims=True) + EPS)
        o_ref[:, hh * V_HEAD:(hh + 1) * V_HEAD] = (
            o_t.T * g[:, hh * V_HEAD:(hh + 1) * V_HEAD]).astype(o_ref.dtype)


def _attention(q_t, k, v_t, gn_b, bsz, seq):
    tq = ATTN_TQ
    tk = v_t.shape[2]
    assert tq == tk
    nq = seq // tq
    hq = MLA_HEADS * QK_PAD
    return pl.pallas_call(
        _attn_kernel,
        grid=(bsz, nq),
        in_specs=[
            pl.BlockSpec((1, hq, tq), lambda b, i: (b * nq + i, 0, 0)),
            pl.BlockSpec((seq, hq), lambda b, i: (b, 0), pipeline_mode=pl.Buffered(1)),
            pl.BlockSpec((seq // tk, MLA_HEADS * V_ROWS, tk), lambda b, i: (b, 0, 0),
                         pipeline_mode=pl.Buffered(1)),
            _resident(gn_b.shape),
        ],
        out_specs=pl.BlockSpec((tq, MLA_WIDTH), lambda b, i: (b * nq + i, 0)),
        out_shape=jax.ShapeDtypeStruct((bsz * seq, MLA_WIDTH), BF16),
        scratch_shapes=[
            pltpu.VMEM((2, tk, tq), F32),
            pltpu.VMEM((MLA_HEADS, 1, tq), F32),
            pltpu.VMEM((MLA_HEADS, V_ROWS, tq), F32),
        ],
        compiler_params=_params(2),
        name="mla_attention",
    )(q_t, k, v_t, gn_b)


def _out_kernel(x_ref, ya_ref, yb_ref, mod_ref, wo_ref, ng_ref, w1_ref, w3_ref, w2_ref, fg_ref, o_ref):
    mod = mod_ref[0]
    y = jnp.concatenate([ya_ref[...], yb_ref[...]], axis=1)
    x = x_ref[...] + mod[5:6] * jnp.dot(y, wo_ref[...], preferred_element_type=F32)
    x = _ffn_update(x, mod[6:7], mod[7:8], mod[8:9], ng_ref[...], w1_ref, w3_ref, w2_ref)
    o_ref[...] = x * _rms_scale(x) * fg_ref[...]


def _out_ffn2(x1, ya, yb, mod, w_out, norm_g, w1, w3, w2, final_g, tiles_per_batch):
    t, d = x1.shape
    tm = ROW_TILE
    return pl.pallas_call(
        _out_kernel,
        grid=(t // tm,),
        in_specs=[
            pl.BlockSpec((tm, d), lambda i: (i, 0)),
            pl.BlockSpec((tm, CONV_WIDTH), lambda i: (i, 0)),
            pl.BlockSpec((tm, MLA_WIDTH), lambda i: (i, 0)),
            pl.BlockSpec((1, N_MOD, d), lambda i: (i // tiles_per_batch, 0, 0)),
            _resident(w_out.shape),
            _resident((1, d)),
            _resident(w1.shape), _resident(w3.shape), _resident(w2.shape),
            _resident((1, d)),
        ],
        out_specs=pl.BlockSpec((tm, d), lambda i: (i, 0)),
        out_shape=jax.ShapeDtypeStruct((t, d), F32),
        compiler_params=_params(),
        name="out_ffn2",
    )(x1, ya, yb, mod, w_out, norm_g, w1, w3, w2, final_g)


def _swap_halves(w, axis):
    lo, hi = jnp.split(w, 2, axis=axis)
    return jnp.concatenate([hi, lo], axis=axis)


def kernel(x, c, positions, ada_w, ada_b, norm_ffn1_g, ffn1_w1, ffn1_w3, ffn1_w2, norm_mix_g, w_in, conv_w, q_norm_g, w_uq, kv_norm_g, w_ukv, out_norm_g, w_out, norm_ffn2_g, ffn2_w1, ffn2_w3, ffn2_w2, final_norm_g):
    bsz, seq, d = x.shape
    depth = ada_w.shape[0]
    t = bsz * seq
    tiles_per_batch = seq // ROW_TILE
    half = QK_ROPE // 2

    cs_t, cs_row = _rope_tables(positions)

    assert depth == 1, "only the one-layer block is implemented"
    l = 0
    xf = x.reshape(t, d)
    mod = _adaln_mod(c, ada_w[l], ada_b[l]).reshape(bsz, N_MOD, d)

    kr0 = 3 * CONV_WIDTH + Q_LORA + KV_LORA
    w_in_p = jnp.concatenate(
        [w_in[l], _swap_halves(w_in[l][:, kr0:kr0 + QK_ROPE], 1)], axis=1).astype(BF16)
    wq = w_uq[l].T.reshape(MLA_HEADS, QK_NOPE + QK_ROPE, Q_LORA)
    wq_rope = wq[:, QK_NOPE:]
    wq_t = jnp.concatenate([wq, _swap_halves(wq_rope, 1)], axis=1).reshape(
        MLA_HEADS * QK_PAD, Q_LORA).astype(BF16)
    wkv = w_ukv[l].reshape(KV_LORA, MLA_HEADS, QK_NOPE + V_HEAD)
    wk = wkv[:, :, :QK_NOPE].reshape(KV_LORA, MLA_HEADS * QK_NOPE).astype(BF16)
    wv_t = wkv[:, :, QK_NOPE:].reshape(KV_LORA, MLA_WIDTH).T.astype(BF16)

    x1 = _ffn1(xf, mod, norm_ffn1_g[l].reshape(1, d), ffn1_w1[l].astype(BF16),
               ffn1_w3[l].astype(BF16), ffn1_w2[l].astype(BF16), tiles_per_batch)
    ya, q_t, k, v_t = _proj(
        x1, mod, norm_mix_g[l].reshape(1, d), w_in_p, conv_w[l],
        out_norm_g[l, :CONV_WIDTH].reshape(1, CONV_WIDTH), q_norm_g[l].reshape(1, Q_LORA), wq_t,
        kv_norm_g[l].reshape(1, KV_LORA), wk, wv_t, cs_row, cs_t, tiles_per_batch)
    yb = _attention(q_t, k, v_t, out_norm_g[l, CONV_WIDTH:].reshape(1, MLA_WIDTH), bsz, seq)
    xf = _out_ffn2(x1, ya, yb, mod, w_out[l].astype(BF16), norm_ffn2_g[l].reshape(1, d),
                   ffn2_w1[l].astype(BF16), ffn2_w3[l].astype(BF16), ffn2_w2[l].astype(BF16),
                   final_norm_g.reshape(1, d), tiles_per_batch)
    return xf.reshape(bsz, seq, d)
```

```python
import functools
import math

import jax
import jax.numpy as jnp
from jax import lax
from jax.experimental import pallas as pl
from jax.experimental.pallas import tpu as pltpu

CHUNK = 64
EPS = 1e-6
N_MOD = 9
CONV_WIDTH = 512
CONV_GROUPS = 8
CONV_K = 3
MLA_HEADS = 4
QK_NOPE = 128
QK_ROPE = 64
V_HEAD = 128
Q_LORA = 384
KV_LORA = 256
ROPE_THETA = 10000.0
MLA_WIDTH = MLA_HEADS * V_HEAD

LANES = 128
SUBLANES = 8
MXU_TILE = 256
QK_PAD = MXU_TILE
BF16_ROWS = 16
V_ROWS = V_HEAD + BF16_ROWS
VMEM_LIMIT_BYTES = 56 * 1024 * 1024

ROW_TILE = 512
ATTN_TQ = 512
ATTN_UNROLL = 4
MOD_TN = 1024
ROPE_TC = 2048

F32 = jnp.float32
BF16 = jnp.bfloat16


def _params(n_axes=1, flags=None):
    return pltpu.CompilerParams(
        dimension_semantics=("arbitrary",) * n_axes,
        vmem_limit_bytes=VMEM_LIMIT_BYTES,
        flags=flags,
    )


def _resident(shape):
    nd = len(shape)
    return pl.BlockSpec(shape, lambda *_: (0,) * nd, pipeline_mode=pl.Buffered(1))


def _rms_scale(x):
    return lax.rsqrt(jnp.mean(x * x, axis=-1, keepdims=True) + EPS)


def _mod_kernel(cb_ref, w_ref, b_ref, o_ref):
    w = w_ref[...]
    tn = w.shape[1]
    for b in range(cb_ref.shape[0]):
        cb = cb_ref[b]
        s = cb * jax.nn.sigmoid(cb)
        cols = [jnp.sum(w[:, j * LANES:(j + 1) * LANES] * s, axis=0, keepdims=True)
                for j in range(tn // LANES)]
        o_ref[b:b + 1, :] = jnp.concatenate(cols, axis=1) + b_ref[...]


def _adaln_mod(c, ada_w, ada_b):
    bsz, d = c.shape
    n = ada_w.shape[1]
    cb = jnp.broadcast_to(c[:, :, None], (bsz, d, LANES))
    return pl.pallas_call(
        _mod_kernel,
        grid=(n // MOD_TN,),
        in_specs=[
            pl.BlockSpec((bsz, d, LANES), lambda j: (0, 0, 0)),
            pl.BlockSpec((d, MOD_TN), lambda j: (0, j)),
            pl.BlockSpec((1, MOD_TN), lambda j: (0, j)),
        ],
        out_specs=pl.BlockSpec((bsz, MOD_TN), lambda j: (0, j)),
        out_shape=jax.ShapeDtypeStruct((bsz, n), F32),
        compiler_params=_params(),
        name="adaln_mod",
    )(cb, ada_w, ada_b.reshape(1, n))


def _cast_streams(weights, n_steps, cols=None):
    cols = cols or [None] * len(weights)
    in_specs, out_specs, out_shapes = [], [], []
    for w, c in zip(weights, cols):
        rows, width = w.shape[0], (c or w.shape[1])
        s = max(k for k in range(1, n_steps + 1)
                if rows % k == 0 and (rows // k) % BF16_ROWS == 0)
        spec = pl.BlockSpec((rows // s, width), lambda i, s=s: (jnp.minimum(i, s - 1), 0))
        in_specs.append(spec)
        out_specs.append(spec)
        out_shapes.append(jax.ShapeDtypeStruct((rows, width), BF16))
    return in_specs, out_specs, out_shapes


def _run_casts(src_refs, dst_refs):
    for src, dst in zip(src_refs, dst_refs):
        dst[...] = src[...].astype(dst.dtype)


def _rope_kernel(pos_ref, inv_ref, *refs):
    n_cast = (len(refs) - 2) // 2
    cst_ref, csr_ref = refs[n_cast], refs[n_cast + 1]
    ang = pos_ref[...].astype(F32) * inv_ref[...]
    c = jnp.cos(ang)
    s = jnp.sin(ang)
    cst = jnp.concatenate([c, c, -s, s], axis=0)
    cst_ref[...] = cst
    csr_ref[...] = cst.T
    _run_casts(refs[:n_cast], refs[n_cast + 2:])


def _rope_tables(positions, cast_weights):
    half = QK_ROPE // 2
    t = positions.size
    tc = ROPE_TC
    n_steps = t // tc
    inv_freq = ROPE_THETA ** (-jnp.arange(0, QK_ROPE, 2, dtype=F32) / QK_ROPE)
    c_in, c_out, c_shapes = _cast_streams(cast_weights, n_steps)
    outs = pl.pallas_call(
        _rope_kernel,
        grid=(n_steps,),
        in_specs=[pl.BlockSpec((1, tc), lambda i: (0, i)),
                  pl.BlockSpec((half, 1), lambda i: (0, 0))] + c_in,
        out_specs=[pl.BlockSpec((4 * half, tc), lambda i: (0, i)),
                   pl.BlockSpec((tc, 4 * half), lambda i: (i, 0))] + c_out,
        out_shape=[jax.ShapeDtypeStruct((4 * half, t), F32),
                   jax.ShapeDtypeStruct((t, 4 * half), F32)] + c_shapes,
        compiler_params=_params(),
        name="rope_tables",
    )(positions.reshape(1, t), inv_freq.reshape(half, 1), *cast_weights)
    return outs[0], outs[1], outs[2:]


def _ffn_update(x, shift, scale, gate, norm_g, w1_ref, w3_ref, w2_ref):
    h = (x * _rms_scale(x) * (norm_g * (1.0 + scale)) + shift).astype(BF16)
    a = jnp.dot(h, w1_ref[...], preferred_element_type=F32)
    b = jnp.dot(h, w3_ref[...], preferred_element_type=F32)
    g = (a * jax.nn.sigmoid(a) * b).astype(BF16)
    o = jnp.dot(g, w2_ref[...], preferred_element_type=F32)
    return x + (0.5 * gate) * o


def _ffn1_kernel(x_ref, mod_ref, ng_ref, w1_ref, w3_ref, w2_ref, *refs):
    n_cast = (len(refs) - 1) // 2
    o_ref = refs[n_cast]
    mod = mod_ref[0]
    o_ref[...] = _ffn_update(x_ref[...], mod[0:1], mod[1:2], mod[2:3], ng_ref[...],
                             w1_ref, w3_ref, w2_ref)
    _run_casts(refs[:n_cast], refs[n_cast + 1:])


def _ffn1(x2d, mod, norm_g, w1, w3, w2, tiles_per_batch, cast_weights, cast_cols):
    t, d = x2d.shape
    tm = ROW_TILE
    n_steps = t // tm
    c_in, c_out, c_shapes = _cast_streams(cast_weights, n_steps, cast_cols)
    outs = pl.pallas_call(
        _ffn1_kernel,
        grid=(n_steps,),
        in_specs=[
            pl.BlockSpec((tm, d), lambda i: (i, 0)),
            pl.BlockSpec((1, N_MOD, d), lambda i: (i // tiles_per_batch, 0, 0)),
            _resident((1, d)),
            _resident(w1.shape), _resident(w3.shape), _resident(w2.shape),
        ] + c_in,
        out_specs=[pl.BlockSpec((tm, d), lambda i: (i, 0))] + c_out,
        out_shape=[jax.ShapeDtypeStruct((t, d), F32)] + c_shapes,
        compiler_params=_params(),
        name="ffn1",
    )(x2d, mod, norm_g, w1, w3, w2, *cast_weights)
    return outs[0], outs[1:]


def _proj_kernel(x_ref, mod_ref, ng_ref, win_ref, wtail_ref, cw_ref, gna_ref, qng_ref, wq_ref, kvng_ref,
                 wk_ref, wv_ref, cs_ref, cst_ref,
                 ya_ref, q_ref, k_ref, v_ref, zbuf_a, zbuf_b, ubuf, *, tiles_per_batch):
    i = pl.program_id(0)

    @pl.when(i == 0)
    def _():
        zbuf_b[...] = jnp.zeros(zbuf_b.shape, F32)
        ubuf[0:SUBLANES, :] = jnp.zeros((SUBLANES, CONV_WIDTH), F32)

    def step(z_new, z_old):
        x = x_ref[...]
        mod = mod_ref[0]
        h = (x * _rms_scale(x) * (ng_ref[...] * (1.0 + mod[4:5])) + mod[3:4]).astype(BF16)
        post = _mixer_post_pieces(z_old, i - 1, cw_ref, gna_ref, qng_ref, wq_ref, kvng_ref, wk_ref,
                                  wv_ref, cs_ref, cst_ref, ya_ref, q_ref, k_ref, v_ref, ubuf,
                                  tiles_per_batch)
        n_main = win_ref.shape[1] // MXU_TILE
        assert wtail_ref.shape[1] == MXU_TILE and len(post) == n_main + 1
        for c in range(n_main + 1):
            cols = slice(c * MXU_TILE, (c + 1) * MXU_TILE)
            w_c = win_ref[:, cols] if c < n_main else wtail_ref[...]
            z_new[:, cols] = jnp.dot(h, w_c, preferred_element_type=F32)
            post[c]()

    pl.when(i % 2 == 0)(functools.partial(step, zbuf_a, zbuf_b))
    pl.when(i % 2 == 1)(functools.partial(step, zbuf_b, zbuf_a))


def _mixer_post_pieces(z, tile, cw_ref, gna_ref, qng_ref, wq_ref, kvng_ref, wk_ref, wv_ref, cs_ref,
                       cst_ref, ya_ref, q_ref, k_ref, v_ref, ubuf, tiles_per_batch):
    tm = z.shape[0]
    cwid = CONV_WIDTH
    c0 = 3 * cwid
    nt = (((1,), (1,)), ((), ()))
    scale = (QK_NOPE + QK_ROPE) ** -0.5 * math.log2(math.e)
    state = {}

    def q_head(hh):
        if hh == 0:
            cq = z[:, c0:c0 + Q_LORA]
            state["cqn"] = (cq * _rms_scale(cq) * qng_ref[...]).astype(BF16)
        base = hh * QK_PAD
        qt = lax.dot_general(wq_ref[base:base + QK_PAD, :], state["cqn"], nt,
                             preferred_element_type=F32)
        ct = cst_ref[0:QK_ROPE, :]
        st = cst_ref[QK_ROPE:2 * QK_ROPE, :]
        rope = qt[QK_NOPE:QK_NOPE + QK_ROPE] * ct + qt[QK_NOPE + QK_ROPE:QK_PAD] * st
        q_ref[0, base:base + QK_PAD, :] = jnp.concatenate(
            [qt[0:QK_NOPE] * scale, rope * scale, jnp.zeros((QK_ROPE, tm), F32)],
            axis=0).astype(q_ref.dtype)

    def kv():
        ckv = z[:, c0 + Q_LORA:c0 + Q_LORA + KV_LORA]
        ckvn = (ckv * _rms_scale(ckv) * kvng_ref[...]).astype(BF16)
        kn = jnp.dot(ckvn, wk_ref[...], preferred_element_type=F32)
        krr = z[:, c0 + Q_LORA + KV_LORA:c0 + Q_LORA + KV_LORA + LANES]
        a = krr * cs_ref[...]
        kr = a + pltpu.roll(a, QK_ROPE, axis=1)
        kp = []
        for hh in range(MLA_HEADS):
            kp += [kn[:, hh * QK_NOPE:(hh + 1) * QK_NOPE], kr]
        k_ref[...] = jnp.concatenate(kp, axis=1).astype(k_ref.dtype)
        vt = lax.dot_general(wv_ref[...], ckvn, nt, preferred_element_type=F32)
        ones = jnp.ones((V_ROWS - V_HEAD, tm), F32)
        vp = []
        for hh in range(MLA_HEADS):
            vp += [vt[hh * V_HEAD:(hh + 1) * V_HEAD], ones]
        v_ref[0] = jnp.concatenate(vp, axis=0).astype(v_ref.dtype)

    def conv_tile(j):
        cols = slice(j * LANES, (j + 1) * LANES)
        ubuf[0:SUBLANES, cols] = jnp.where(tile % tiles_per_batch == 0, 0.0, ubuf[0:SUBLANES, cols])
        u = z[:, cwid + j * LANES:cwid + (j + 1) * LANES] * z[:, 2 * cwid + j * LANES:2 * cwid + (j + 1) * LANES]
        ubuf[SUBLANES:SUBLANES + tm, cols] = u
        u1 = ubuf[SUBLANES - 1:SUBLANES - 1 + tm, cols]
        u2 = ubuf[SUBLANES - 2:SUBLANES - 2 + tm, cols]
        cw = cw_ref[:, cols]
        ya = z[:, cols] * (cw[0:1] * u2 + cw[1:2] * u1 + cw[2:3] * u)
        ubuf[0:SUBLANES, cols] = ubuf[tm:tm + SUBLANES, cols]
        gw = cwid // CONV_GROUPS
        low = lax.broadcasted_iota(jnp.int32, (tm, LANES), 1) < gw
        sq = ya * ya
        s_lo = jnp.sum(jnp.where(low, sq, 0.0), axis=-1, keepdims=True)
        s_hi = jnp.sum(jnp.where(low, 0.0, sq), axis=-1, keepdims=True)
        ms = jnp.where(low, s_lo, s_hi) * (1.0 / gw)
        ya_ref[:, cols] = (ya * lax.rsqrt(ms + EPS) * gna_ref[:, cols]).astype(ya_ref.dtype)

    pieces = [functools.partial(q_head, hh) for hh in range(MLA_HEADS)] + [kv]
    pieces += [functools.partial(conv_tile, j) for j in range(cwid // LANES)]
    return pieces


def _proj(x1, mod, norm_g, w_main, w_tail, conv_w, gn_a, q_norm_g, wq_t, kv_norm_g, wk, wv_t, cs_row, cs_t,
          tiles_per_batch):
    t, d = x1.shape
    tm = ROW_TILE
    nt = t // tm
    hq = MLA_HEADS * QK_PAD

    def cur(i):
        return jnp.minimum(i, nt - 1)

    def prev(i):
        return jnp.maximum(i - 1, 0)

    return pl.pallas_call(
        functools.partial(_proj_kernel, tiles_per_batch=tiles_per_batch),
        grid=(nt + 1,),
        in_specs=[
            pl.BlockSpec((tm, d), lambda i: (cur(i), 0)),
            pl.BlockSpec((1, N_MOD, d), lambda i: (cur(i) // tiles_per_batch, 0, 0)),
            _resident((1, d)),
            _resident(w_main.shape),
            _resident(w_tail.shape),
            _resident(conv_w.shape),
            _resident(gn_a.shape),
            _resident(q_norm_g.shape),
            _resident(wq_t.shape),
            _resident(kv_norm_g.shape),
            _resident(wk.shape),
            _resident(wv_t.shape),
            pl.BlockSpec((tm, LANES), lambda i: (prev(i), 0)),
            pl.BlockSpec((LANES, tm), lambda i: (0, prev(i))),
        ],
        out_specs=[
            pl.BlockSpec((tm, CONV_WIDTH), lambda i: (prev(i), 0)),
            pl.BlockSpec((1, hq, tm), lambda i: (prev(i), 0, 0)),
            pl.BlockSpec((tm, hq), lambda i: (prev(i), 0)),
            pl.BlockSpec((1, MLA_HEADS * V_ROWS, tm), lambda i: (prev(i), 0, 0)),
        ],
        out_shape=[
            jax.ShapeDtypeStruct((t, CONV_WIDTH), BF16),
            jax.ShapeDtypeStruct((nt, hq, tm), BF16),
            jax.ShapeDtypeStruct((t, hq), BF16),
            jax.ShapeDtypeStruct((nt, MLA_HEADS * V_ROWS, tm), BF16),
        ],
        scratch_shapes=[pltpu.VMEM((tm, w_main.shape[1] + w_tail.shape[1]), F32),
                        pltpu.VMEM((tm, w_main.shape[1] + w_tail.shape[1]), F32),
                        pltpu.VMEM((tm + SUBLANES, CONV_WIDTH), F32)],
        compiler_params=_params(),
        name="mixer_proj",
    )(x1, mod, norm_g, w_main, w_tail, conv_w, gn_a, q_norm_g, wq_t, kv_norm_g, wk, wv_t, cs_row, cs_t)


def _attn_kernel(q_ref, k_ref, v_ref, g_ref, o_ref, s_sc, m_sc, acc_sc):
    i = pl.program_id(1)
    tq = q_ref.shape[2]
    tk = v_ref.shape[2]
    m_sc[...] = jnp.full(m_sc.shape, -jnp.inf, F32)
    acc_sc[...] = jnp.zeros(acc_sc.shape, F32)

    def scores(j, hh, slot):
        row0 = pl.multiple_of(j * tk, tk)
        s_sc[slot] = jnp.dot(k_ref[pl.ds(row0, tk), hh * QK_PAD:(hh + 1) * QK_PAD],
                             q_ref[0, hh * QK_PAD:(hh + 1) * QK_PAD, :],
                             preferred_element_type=F32)

    def softmax_pv(j, hh, slot, masked):
        s = s_sc[slot]
        if masked:
            kc = lax.broadcasted_iota(jnp.int32, (tk, tq), 0) // CHUNK
            qc = lax.broadcasted_iota(jnp.int32, (tk, tq), 1) // CHUNK
            s = jnp.where(kc <= qc, s, -1e30)
        m_prev = m_sc[hh]
        m_new = jnp.maximum(m_prev, jnp.max(s, axis=0, keepdims=True))
        alpha = jnp.exp2(m_prev - m_new)
        p = jnp.exp2(s - m_new).astype(BF16)
        pv = jnp.dot(v_ref[j, hh * V_ROWS:(hh + 1) * V_ROWS, :], p,
                     preferred_element_type=F32)
        acc_sc[hh] = alpha * acc_sc[hh] + pv
        m_sc[hh] = m_new

    scores(0, 0, 0)

    def block(j, masked, next_j):
        for hh in range(MLA_HEADS):
            if hh + 1 < MLA_HEADS:
                scores(j, hh + 1, (hh + 1) % 2)
            elif next_j is not None:
                scores(next_j, 0, 0)
            softmax_pv(j, hh, hh % 2, masked)

    def run(j0, n):
        for t in range(n):
            block(j0 + t, False, j0 + t + 1)

    n_quads = lax.shift_right_logical(i, 2)

    def quad(q, carry):
        run(q * ATTN_UNROLL, ATTN_UNROLL)
        return carry

    lax.fori_loop(0, n_quads, quad, 0)
    done = n_quads * ATTN_UNROLL

    @pl.when((i & 2) != 0)
    def _():
        run(done, 2)

    @pl.when((i & 1) != 0)
    def _():
        run(done + (i & 2), 1)

    block(i, True, None)

    g = g_ref[...]
    for hh in range(MLA_HEADS):
        acc = acc_sc[hh]
        o_t = acc[0:V_HEAD] / acc[V_HEAD:V_HEAD + 1]
        o_t = o_t * lax.rsqrt(jnp.mean(o_t * o_t, axis=0, keepdims=True) + EPS)
        o_ref[:, hh * V_HEAD:(hh + 1) * V_HEAD] = (
            o_t.T * g[:, hh * V_HEAD:(hh + 1) * V_HEAD]).astype(o_ref.dtype)


def _attention(q_t, k, v_t, gn_b, bsz, seq):
    tq = ATTN_TQ
    tk = v_t.shape[2]
    assert tq == tk
    nq = seq // tq
    hq = MLA_HEADS * QK_PAD
    return pl.pallas_call(
        _attn_kernel,
        grid=(bsz, nq),
        in_specs=[
            pl.BlockSpec((1, hq, tq), lambda b, i: (b * nq + i, 0, 0)),
            pl.BlockSpec((seq, hq), lambda b, i: (b, 0), pipeline_mode=pl.Buffered(1)),
            pl.BlockSpec((seq // tk, MLA_HEADS * V_ROWS, tk), lambda b, i: (b, 0, 0),
                         pipeline_mode=pl.Buffered(1)),
            _resident(gn_b.shape),
        ],
        out_specs=pl.BlockSpec((tq, MLA_WIDTH), lambda b, i: (b * nq + i, 0)),
        out_shape=jax.ShapeDtypeStruct((bsz * seq, MLA_WIDTH), BF16),
        scratch_shapes=[
            pltpu.VMEM((2, tk, tq), F32),
            pltpu.VMEM((MLA_HEADS, 1, tq), F32),
            pltpu.VMEM((MLA_HEADS, V_ROWS, tq), F32),
        ],
        compiler_params=_params(2),
        name="mla_attention",
    )(q_t, k, v_t, gn_b)


def _out_kernel(x_ref, ya_ref, yb_ref, mod_ref, wo_ref, ng_ref, w1_ref, w3_ref, w2_ref, fg_ref, o_ref):
    mod = mod_ref[0]
    y = jnp.concatenate([ya_ref[...], yb_ref[...]], axis=1)
    x = x_ref[...] + mod[5:6] * jnp.dot(y, wo_ref[...], preferred_element_type=F32)
    x = _ffn_update(x, mod[6:7], mod[7:8], mod[8:9], ng_ref[...], w1_ref, w3_ref, w2_ref)
    o_ref[...] = x * _rms_scale(x) * fg_ref[...]


def _out_ffn2(x1, ya, yb, mod, w_out, norm_g, w1, w3, w2, final_g, tiles_per_batch):
    t, d = x1.shape
    tm = ROW_TILE
    return pl.pallas_call(
        _out_kernel,
        grid=(t // tm,),
        in_specs=[
            pl.BlockSpec((tm, d), lambda i: (i, 0)),
            pl.BlockSpec((tm, CONV_WIDTH), lambda i: (i, 0)),
            pl.BlockSpec((tm, MLA_WIDTH), lambda i: (i, 0)),
            pl.BlockSpec((1, N_MOD, d), lambda i: (i // tiles_per_batch, 0, 0)),
            _resident(w_out.shape),
            _resident((1, d)),
            _resident(w1.shape), _resident(w3.shape), _resident(w2.shape),
            _resident((1, d)),
        ],
        out_specs=pl.BlockSpec((tm, d), lambda i: (i, 0)),
        out_shape=jax.ShapeDtypeStruct((t, d), F32),
        compiler_params=_params(),
        name="out_ffn2",
    )(x1, ya, yb, mod, w_out, norm_g, w1, w3, w2, final_g)


def _swap_halves(w, axis):
    lo, hi = jnp.split(w, 2, axis=axis)
    return jnp.concatenate([hi, lo], axis=axis)


def kernel(x, c, positions, ada_w, ada_b, norm_ffn1_g, ffn1_w1, ffn1_w3, ffn1_w2, norm_mix_g, w_in, conv_w, q_norm_g, w_uq, kv_norm_g, w_ukv, out_norm_g, w_out, norm_ffn2_g, ffn2_w1, ffn2_w3, ffn2_w2, final_norm_g):
    bsz, seq, d = x.shape
    depth = ada_w.shape[0]
    t = bsz * seq
    tiles_per_batch = seq // ROW_TILE
    half = QK_ROPE // 2

    assert depth == 1, "only the one-layer block is implemented"
    l = 0

    cs_t, cs_row, (f1w1, f1w3, f1w2) = _rope_tables(positions, [ffn1_w1[l], ffn1_w3[l], ffn1_w2[l]])

    xf = x.reshape(t, d)
    mod = _adaln_mod(c, ada_w[l], ada_b[l]).reshape(bsz, N_MOD, d)

    kr0 = 3 * CONV_WIDTH + Q_LORA + KV_LORA
    n_main = (w_in.shape[2] // MXU_TILE) * MXU_TILE
    w_tail = jnp.concatenate(
        [w_in[l][:, n_main:], _swap_halves(w_in[l][:, kr0:kr0 + QK_ROPE], 1)], axis=1)
    wq = w_uq[l].T.reshape(MLA_HEADS, QK_NOPE + QK_ROPE, Q_LORA)
    wq_rope = wq[:, QK_NOPE:]
    wq_t = jnp.concatenate([wq, _swap_halves(wq_rope, 1)], axis=1).reshape(
        MLA_HEADS * QK_PAD, Q_LORA).astype(BF16)
    wkv = w_ukv[l].reshape(KV_LORA, MLA_HEADS, QK_NOPE + V_HEAD)
    wk = wkv[:, :, :QK_NOPE].reshape(KV_LORA, MLA_HEADS * QK_NOPE).astype(BF16)
    wv_t = wkv[:, :, QK_NOPE:].reshape(KV_LORA, MLA_WIDTH).T.astype(BF16)

    x1, (w_main, w_tail, wo, f2w1, f2w3, f2w2) = _ffn1(
        xf, mod, norm_ffn1_g[l].reshape(1, d), f1w1, f1w3, f1w2, tiles_per_batch,
        [w_in[l], w_tail, w_out[l], ffn2_w1[l], ffn2_w3[l], ffn2_w2[l]],
        [n_main, None, None, None, None, None])
    ya, q_t, k, v_t = _proj(
        x1, mod, norm_mix_g[l].reshape(1, d), w_main, w_tail, conv_w[l],
        out_norm_g[l, :CONV_WIDTH].reshape(1, CONV_WIDTH), q_norm_g[l].reshape(1, Q_LORA), wq_t,
        kv_norm_g[l].reshape(1, KV_LORA), wk, wv_t, cs_row, cs_t, tiles_per_batch)
    yb = _attention(q_t, k, v_t, out_norm_g[l, CONV_WIDTH:].reshape(1, MLA_WIDTH), bsz, seq)
    xf = _out_ffn2(x1, ya, yb, mod, wo, norm_ffn2_g[l].reshape(1, d), f2w1, f2w3, f2w2,
                   final_norm_g.reshape(1, d), tiles_per_batch)
    return xf.reshape(bsz, seq, d)
```

```python
import functools
import math

import jax
import jax.numpy as jnp
from jax import lax
from jax.experimental import pallas as pl
from jax.experimental.pallas import tpu as pltpu

CHUNK = 64
EPS = 1e-6
N_MOD = 9
CONV_WIDTH = 512
CONV_GROUPS = 8
CONV_K = 3
MLA_HEADS = 4
QK_NOPE = 128
QK_ROPE = 64
V_HEAD = 128
Q_LORA = 384
KV_LORA = 256
ROPE_THETA = 10000.0
MLA_WIDTH = MLA_HEADS * V_HEAD

LANES = 128
SUBLANES = 8
MXU_TILE = 256
QK_PAD = MXU_TILE
BF16_ROWS = 16
V_ROWS = V_HEAD + BF16_ROWS
VMEM_LIMIT_BYTES = 56 * 1024 * 1024

ROW_TILE = 512
ATTN_TQ = 512
ATTN_UNROLL = 4
MOD_TN = 1024
ROPE_TC = 2048

F32 = jnp.float32
BF16 = jnp.bfloat16


def _params(n_axes=1, flags=None):
    return pltpu.CompilerParams(
        dimension_semantics=("arbitrary",) * n_axes,
        vmem_limit_bytes=VMEM_LIMIT_BYTES,
        flags=flags,
    )


def _resident(shape):
    nd = len(shape)
    return pl.BlockSpec(shape, lambda *_: (0,) * nd, pipeline_mode=pl.Buffered(1))


def _rms_scale(x):
    return lax.rsqrt(jnp.mean(x * x, axis=-1, keepdims=True) + EPS)


def _mod_kernel(cb_ref, w_ref, b_ref, o_ref):
    w = w_ref[...]
    tn = w.shape[1]
    for b in range(cb_ref.shape[0]):
        cb = cb_ref[b]
        s = cb * jax.nn.sigmoid(cb)
        cols = [jnp.sum(w[:, j * LANES:(j + 1) * LANES] * s, axis=0, keepdims=True)
                for j in range(tn // LANES)]
        o_ref[b:b + 1, :] = jnp.concatenate(cols, axis=1) + b_ref[...]


def _adaln_mod(c, ada_w, ada_b):
    bsz, d = c.shape
    n = ada_w.shape[1]
    cb = jnp.broadcast_to(c[:, :, None], (bsz, d, LANES))
    return pl.pallas_call(
        _mod_kernel,
        grid=(n // MOD_TN,),
        in_specs=[
            pl.BlockSpec((bsz, d, LANES), lambda j: (0, 0, 0)),
            pl.BlockSpec((d, MOD_TN), lambda j: (0, j)),
            pl.BlockSpec((1, MOD_TN), lambda j: (0, j)),
        ],
        out_specs=pl.BlockSpec((bsz, MOD_TN), lambda j: (0, j)),
        out_shape=jax.ShapeDtypeStruct((bsz, n), F32),
        compiler_params=_params(),
        name="adaln_mod",
    )(cb, ada_w, ada_b.reshape(1, n))


def _cast_streams(weights, n_steps):
    in_specs, out_specs, out_shapes = [], [], []
    for w in weights:
        rows, width = w.shape
        s = max(k for k in range(1, n_steps + 1)
                if rows % k == 0 and (rows // k) % BF16_ROWS == 0)
        spec = pl.BlockSpec((rows // s, width), lambda i, s=s: (jnp.minimum(i, s - 1), 0))
        in_specs.append(spec)
        out_specs.append(spec)
        out_shapes.append(jax.ShapeDtypeStruct((rows, width), BF16))
    return in_specs, out_specs, out_shapes


def _run_casts(src_refs, dst_refs):
    for src, dst in zip(src_refs, dst_refs):
        dst[...] = src[...].astype(dst.dtype)


def _rope_kernel(pos_ref, inv_ref, *refs):
    n_cast = (len(refs) - 2) // 2
    cst_ref, csr_ref = refs[n_cast], refs[n_cast + 1]
    ang = pos_ref[...].astype(F32) * inv_ref[...]
    c = jnp.cos(ang)
    s = jnp.sin(ang)
    cst = jnp.concatenate([c, c, -s, s], axis=0)
    cst_ref[...] = cst
    csr_ref[...] = cst.T
    _run_casts(refs[:n_cast], refs[n_cast + 2:])


def _rope_tables(positions, cast_weights):
    half = QK_ROPE // 2
    t = positions.size
    tc = ROPE_TC
    n_steps = t // tc
    inv_freq = ROPE_THETA ** (-jnp.arange(0, QK_ROPE, 2, dtype=F32) / QK_ROPE)
    c_in, c_out, c_shapes = _cast_streams(cast_weights, n_steps)
    outs = pl.pallas_call(
        _rope_kernel,
        grid=(n_steps,),
        in_specs=[pl.BlockSpec((1, tc), lambda i: (0, i)),
                  pl.BlockSpec((half, 1), lambda i: (0, 0))] + c_in,
        out_specs=[pl.BlockSpec((4 * half, tc), lambda i: (0, i)),
                   pl.BlockSpec((tc, 4 * half), lambda i: (i, 0))] + c_out,
        out_shape=[jax.ShapeDtypeStruct((4 * half, t), F32),
                   jax.ShapeDtypeStruct((t, 4 * half), F32)] + c_shapes,
        compiler_params=_params(),
        name="rope_tables",
    )(positions.reshape(1, t), inv_freq.reshape(half, 1), *cast_weights)
    return outs[0], outs[1], outs[2:]


def _ffn_update(x, shift, scale, gate, norm_g, w1_ref, w3_ref, w2_ref):
    h = (x * _rms_scale(x) * (norm_g * (1.0 + scale)) + shift).astype(BF16)
    a = jnp.dot(h, w1_ref[...], preferred_element_type=F32)
    b = jnp.dot(h, w3_ref[...], preferred_element_type=F32)
    g = (a * jax.nn.sigmoid(a) * b).astype(BF16)
    o = jnp.dot(g, w2_ref[...], preferred_element_type=F32)
    return x + (0.5 * gate) * o


def _ffn1_kernel(x_ref, mod_ref, ng_ref, w1_ref, w3_ref, w2_ref, win_ref, *refs):
    n_cast = (len(refs) - 3) // 2
    o_ref, wmain_ref, wtail_ref = refs[n_cast:n_cast + 3]
    mod = mod_ref[0]
    o_ref[...] = _ffn_update(x_ref[...], mod[0:1], mod[1:2], mod[2:3], ng_ref[...],
                             w1_ref, w3_ref, w2_ref)
    _run_casts(refs[:n_cast], refs[n_cast + 3:])
    w = win_ref[...]
    n_main = wmain_ref.shape[1]
    kr0 = w.shape[1] - QK_ROPE
    half = QK_ROPE // 2
    wmain_ref[...] = w[:, :n_main].astype(BF16)
    wtail_ref[...] = jnp.concatenate(
        [w[:, n_main:], w[:, kr0 + half:], w[:, kr0:kr0 + half]], axis=1).astype(BF16)


def _ffn1(x2d, mod, norm_g, w1, w3, w2, tiles_per_batch, w_in, cast_weights):
    t, d = x2d.shape
    tm = ROW_TILE
    n_steps = t // tm
    c_in, c_out, c_shapes = _cast_streams(cast_weights, n_steps)
    rows, in_cols = w_in.shape
    n_main = (in_cols // MXU_TILE) * MXU_TILE
    assert in_cols - n_main + QK_ROPE == MXU_TILE and rows % n_steps == 0
    wr = rows // n_steps
    outs = pl.pallas_call(
        _ffn1_kernel,
        grid=(n_steps,),
        in_specs=[
            pl.BlockSpec((tm, d), lambda i: (i, 0)),
            pl.BlockSpec((1, N_MOD, d), lambda i: (i // tiles_per_batch, 0, 0)),
            _resident((1, d)),
            _resident(w1.shape), _resident(w3.shape), _resident(w2.shape),
            pl.BlockSpec((wr, in_cols), lambda i: (i, 0)),
        ] + c_in,
        out_specs=[pl.BlockSpec((tm, d), lambda i: (i, 0)),
                   pl.BlockSpec((wr, n_main), lambda i: (i, 0)),
                   pl.BlockSpec((wr, MXU_TILE), lambda i: (i, 0))] + c_out,
        out_shape=[jax.ShapeDtypeStruct((t, d), F32),
                   jax.ShapeDtypeStruct((rows, n_main), BF16),
                   jax.ShapeDtypeStruct((rows, MXU_TILE), BF16)] + c_shapes,
        compiler_params=_params(),
        name="ffn1",
    )(x2d, mod, norm_g, w1, w3, w2, w_in, *cast_weights)
    return outs[0], outs[1], outs[2], outs[3:]


def _proj_kernel(x_ref, mod_ref, ng_ref, win_ref, wtail_ref, cw_ref, gna_ref, qng_ref, wq_ref, kvng_ref,
                 wk_ref, wv_ref, cs_ref, cst_ref,
                 ya_ref, q_ref, k_ref, v_ref, zbuf_a, zbuf_b, ubuf, *, tiles_per_batch):
    i = pl.program_id(0)

    @pl.when(i == 0)
    def _():
        zbuf_b[...] = jnp.zeros(zbuf_b.shape, F32)
        ubuf[0:SUBLANES, :] = jnp.zeros((SUBLANES, CONV_WIDTH), F32)

    def step(z_new, z_old):
        x = x_ref[...]
        mod = mod_ref[0]
        h = (x * _rms_scale(x) * (ng_ref[...] * (1.0 + mod[4:5])) + mod[3:4]).astype(BF16)
        post = _mixer_post_pieces(z_old, i - 1, cw_ref, gna_ref, qng_ref, wq_ref, kvng_ref, wk_ref,
                                  wv_ref, cs_ref, cst_ref, ya_ref, q_ref, k_ref, v_ref, ubuf,
                                  tiles_per_batch)
        n_main = win_ref.shape[1] // MXU_TILE
        assert wtail_ref.shape[1] == MXU_TILE and len(post) == n_main + 1
        for c in range(n_main + 1):
            cols = slice(c * MXU_TILE, (c + 1) * MXU_TILE)
            w_c = win_ref[:, cols] if c < n_main else wtail_ref[...]
            z_new[:, cols] = jnp.dot(h, w_c, preferred_element_type=F32)
            post[c]()

    pl.when(i % 2 == 0)(functools.partial(step, zbuf_a, zbuf_b))
    pl.when(i % 2 == 1)(functools.partial(step, zbuf_b, zbuf_a))


def _mixer_post_pieces(z, tile, cw_ref, gna_ref, qng_ref, wq_ref, kvng_ref, wk_ref, wv_ref, cs_ref,
                       cst_ref, ya_ref, q_ref, k_ref, v_ref, ubuf, tiles_per_batch):
    tm = z.shape[0]
    cwid = CONV_WIDTH
    c0 = 3 * cwid
    nt = (((1,), (1,)), ((), ()))
    scale = (QK_NOPE + QK_ROPE) ** -0.5 * math.log2(math.e)
    state = {}

    def q_head(hh):
        if hh == 0:
            cq = z[:, c0:c0 + Q_LORA]
            state["cqn"] = (cq * _rms_scale(cq) * qng_ref[...]).astype(BF16)
        base = hh * QK_PAD
        qt = lax.dot_general(wq_ref[base:base + QK_PAD, :], state["cqn"], nt,
                             preferred_element_type=F32)
        ct = cst_ref[0:QK_ROPE, :]
        st = cst_ref[QK_ROPE:2 * QK_ROPE, :]
        rope = qt[QK_NOPE:QK_NOPE + QK_ROPE] * ct + qt[QK_NOPE + QK_ROPE:QK_PAD] * st
        q_ref[0, base:base + QK_PAD, :] = jnp.concatenate(
            [qt[0:QK_NOPE] * scale, rope * scale, jnp.zeros((QK_ROPE, tm), F32)],
            axis=0).astype(q_ref.dtype)

    def kv():
        ckv = z[:, c0 + Q_LORA:c0 + Q_LORA + KV_LORA]
        ckvn = (ckv * _rms_scale(ckv) * kvng_ref[...]).astype(BF16)
        kn = jnp.dot(ckvn, wk_ref[...], preferred_element_type=F32)
        krr = z[:, c0 + Q_LORA + KV_LORA:c0 + Q_LORA + KV_LORA + LANES]
        a = krr * cs_ref[...]
        kr = a + pltpu.roll(a, QK_ROPE, axis=1)
        kp = []
        for hh in range(MLA_HEADS):
            kp += [kn[:, hh * QK_NOPE:(hh + 1) * QK_NOPE], kr]
        k_ref[...] = jnp.concatenate(kp, axis=1).astype(k_ref.dtype)
        vt = lax.dot_general(wv_ref[...], ckvn, nt, preferred_element_type=F32)
        ones = jnp.ones((V_ROWS - V_HEAD, tm), F32)
        vp = []
        for hh in range(MLA_HEADS):
            vp += [vt[hh * V_HEAD:(hh + 1) * V_HEAD], ones]
        v_ref[0] = jnp.concatenate(vp, axis=0).astype(v_ref.dtype)

    def conv_tile(j):
        cols = slice(j * LANES, (j + 1) * LANES)
        ubuf[0:SUBLANES, cols] = jnp.where(tile % tiles_per_batch == 0, 0.0, ubuf[0:SUBLANES, cols])
        u = z[:, cwid + j * LANES:cwid + (j + 1) * LANES] * z[:, 2 * cwid + j * LANES:2 * cwid + (j + 1) * LANES]
        ubuf[SUBLANES:SUBLANES + tm, cols] = u
        u1 = ubuf[SUBLANES - 1:SUBLANES - 1 + tm, cols]
        u2 = ubuf[SUBLANES - 2:SUBLANES - 2 + tm, cols]
        cw = cw_ref[:, cols]
        ya = z[:, cols] * (cw[0:1] * u2 + cw[1:2] * u1 + cw[2:3] * u)
        ubuf[0:SUBLANES, cols] = ubuf[tm:tm + SUBLANES, cols]
        gw = cwid // CONV_GROUPS
        low = lax.broadcasted_iota(jnp.int32, (tm, LANES), 1) < gw
        sq = ya * ya
        s_lo = jnp.sum(jnp.where(low, sq, 0.0), axis=-1, keepdims=True)
        s_hi = jnp.sum(jnp.where(low, 0.0, sq), axis=-1, keepdims=True)
        ms = jnp.where(low, s_lo, s_hi) * (1.0 / gw)
        ya_ref[:, cols] = (ya * lax.rsqrt(ms + EPS) * gna_ref[:, cols]).astype(ya_ref.dtype)

    pieces = [functools.partial(q_head, hh) for hh in range(MLA_HEADS)] + [kv]
    pieces += [functools.partial(conv_tile, j) for j in range(cwid // LANES)]
    return pieces


def _proj(x1, mod, norm_g, w_main, w_tail, conv_w, gn_a, q_norm_g, wq_t, kv_norm_g, wk, wv_t, cs_row, cs_t,
          tiles_per_batch):
    t, d = x1.shape
    tm = ROW_TILE
    nt = t // tm
    hq = MLA_HEADS * QK_PAD

    def cur(i):
        return jnp.minimum(i, nt - 1)

    def prev(i):
        return jnp.maximum(i - 1, 0)

    return pl.pallas_call(
        functools.partial(_proj_kernel, tiles_per_batch=tiles_per_batch),
        grid=(nt + 1,),
        in_specs=[
            pl.BlockSpec((tm, d), lambda i: (cur(i), 0)),
            pl.BlockSpec((1, N_MOD, d), lambda i: (cur(i) // tiles_per_batch, 0, 0)),
            _resident((1, d)),
            _resident(w_main.shape),
            _resident(w_tail.shape),
            _resident(conv_w.shape),
            _resident(gn_a.shape),
            _resident(q_norm_g.shape),
            _resident(wq_t.shape),
            _resident(kv_norm_g.shape),
            _resident(wk.shape),
            _resident(wv_t.shape),
            pl.BlockSpec((tm, LANES), lambda i: (prev(i), 0)),
            pl.BlockSpec((LANES, tm), lambda i: (0, prev(i))),
        ],
        out_specs=[
            pl.BlockSpec((tm, CONV_WIDTH), lambda i: (prev(i), 0)),
            pl.BlockSpec((1, hq, tm), lambda i: (prev(i), 0, 0)),
            pl.BlockSpec((tm, hq), lambda i: (prev(i), 0)),
            pl.BlockSpec((1, MLA_HEADS * V_ROWS, tm), lambda i: (prev(i), 0, 0)),
        ],
        out_shape=[
            jax.ShapeDtypeStruct((t, CONV_WIDTH), BF16),
            jax.ShapeDtypeStruct((nt, hq, tm), BF16),
            jax.ShapeDtypeStruct((t, hq), BF16),
            jax.ShapeDtypeStruct((nt, MLA_HEADS * V_ROWS, tm), BF16),
        ],
        scratch_shapes=[pltpu.VMEM((tm, w_main.shape[1] + w_tail.shape[1]), F32),
                        pltpu.VMEM((tm, w_main.shape[1] + w_tail.shape[1]), F32),
                        pltpu.VMEM((tm + SUBLANES, CONV_WIDTH), F32)],
        compiler_params=_params(),
        name="mixer_proj",
    )(x1, mod, norm_g, w_main, w_tail, conv_w, gn_a, q_norm_g, wq_t, kv_norm_g, wk, wv_t, cs_row, cs_t)


def _attn_kernel(q_ref, k_ref, v_ref, g_ref, o_ref, s_sc, m_sc, acc_sc):
    i = pl.program_id(1)
    tq = q_ref.shape[2]
    tk = v_ref.shape[2]
    m_sc[...] = jnp.full(m_sc.shape, -jnp.inf, F32)
    acc_sc[...] = jnp.zeros(acc_sc.shape, F32)

    def scores(j, hh, slot):
        row0 = pl.multiple_of(j * tk, tk)
        s_sc[slot] = jnp.dot(k_ref[pl.ds(row0, tk), hh * QK_PAD:(hh + 1) * QK_PAD],
                             q_ref[0, hh * QK_PAD:(hh + 1) * QK_PAD, :],
                             preferred_element_type=F32)

    def softmax_pv(j, hh, slot, masked):
        s = s_sc[slot]
        if masked:
            kc = lax.broadcasted_iota(jnp.int32, (tk, tq), 0) // CHUNK
            qc = lax.broadcasted_iota(jnp.int32, (tk, tq), 1) // CHUNK
            s = jnp.where(kc <= qc, s, -1e30)
        m_prev = m_sc[hh]
        m_new = jnp.maximum(m_prev, jnp.max(s, axis=0, keepdims=True))
        alpha = jnp.exp2(m_prev - m_new)
        p = jnp.exp2(s - m_new).astype(BF16)
        pv = jnp.dot(v_ref[j, hh * V_ROWS:(hh + 1) * V_ROWS, :], p,
                     preferred_element_type=F32)
        acc_sc[hh] = alpha * acc_sc[hh] + pv
        m_sc[hh] = m_new

    scores(0, 0, 0)

    def block(j, masked, next_j):
        for hh in range(MLA_HEADS):
            if hh + 1 < MLA_HEADS:
                scores(j, hh + 1, (hh + 1) % 2)
            elif next_j is not None:
                scores(next_j, 0, 0)
            softmax_pv(j, hh, hh % 2, masked)

    def run(j0, n):
        for t in range(n):
            block(j0 + t, False, j0 + t + 1)

    n_quads = lax.shift_right_logical(i, 2)

    def quad(q, carry):
        run(q * ATTN_UNROLL, ATTN_UNROLL)
        return carry

    lax.fori_loop(0, n_quads, quad, 0)
    done = n_quads * ATTN_UNROLL

    @pl.when((i & 2) != 0)
    def _():
        run(done, 2)

    @pl.when((i & 1) != 0)
    def _():
        run(done + (i & 2), 1)

    block(i, True, None)

    g = g_ref[...]
    for hh in range(MLA_HEADS):
        acc = acc_sc[hh]
        o_t = acc[0:V_HEAD] / acc[V_HEAD:V_HEAD + 1]
        o_t = o_t * lax.rsqrt(jnp.mean(o_t * o_t, axis=0, keepdims=True) + EPS)
        o_ref[:, hh * V_HEAD:(hh + 1) * V_HEAD] = (
            o_t.T * g[:, hh * V_HEAD:(hh + 1) * V_HEAD]).astype(o_ref.dtype)


def _attention(q_t, k, v_t, gn_b, bsz, seq):
    tq = ATTN_TQ
    tk = v_t.shape[2]
    assert tq == tk
    nq = seq // tq
    hq = MLA_HEADS * QK_PAD
    return pl.pallas_call(
        _attn_kernel,
        grid=(bsz, nq),
        in_specs=[
            pl.BlockSpec((1, hq, tq), lambda b, i: (b * nq + i, 0, 0)),
            pl.BlockSpec((seq, hq), lambda b, i: (b, 0), pipeline_mode=pl.Buffered(1)),
            pl.BlockSpec((seq // tk, MLA_HEADS * V_ROWS, tk), lambda b, i: (b, 0, 0),
                         pipeline_mode=pl.Buffered(1)),
            _resident(gn_b.shape),
        ],
        out_specs=pl.BlockSpec((tq, MLA_WIDTH), lambda b, i: (b * nq + i, 0)),
        out_shape=jax.ShapeDtypeStruct((bsz * seq, MLA_WIDTH), BF16),
        scratch_shapes=[
            pltpu.VMEM((2, tk, tq), F32),
            pltpu.VMEM((MLA_HEADS, 1, tq), F32),
            pltpu.VMEM((MLA_HEADS, V_ROWS, tq), F32),
        ],
        compiler_params=_params(2),
        name="mla_attention",
    )(q_t, k, v_t, gn_b)


def _out_kernel(x_ref, ya_ref, yb_ref, mod_ref, wo_ref, ng_ref, w1_ref, w3_ref, w2_ref, fg_ref, o_ref):
    mod = mod_ref[0]
    y = jnp.concatenate([ya_ref[...], yb_ref[...]], axis=1)
    x = x_ref[...] + mod[5:6] * jnp.dot(y, wo_ref[...], preferred_element_type=F32)
    x = _ffn_update(x, mod[6:7], mod[7:8], mod[8:9], ng_ref[...], w1_ref, w3_ref, w2_ref)
    o_ref[...] = x * _rms_scale(x) * fg_ref[...]


def _out_ffn2(x1, ya, yb, mod, w_out, norm_g, w1, w3, w2, final_g, tiles_per_batch):
    t, d = x1.shape
    tm = ROW_TILE
    return pl.pallas_call(
        _out_kernel,
        grid=(t // tm,),
        in_specs=[
            pl.BlockSpec((tm, d), lambda i: (i, 0)),
            pl.BlockSpec((tm, CONV_WIDTH), lambda i: (i, 0)),
            pl.BlockSpec((tm, MLA_WIDTH), lambda i: (i, 0)),
            pl.BlockSpec((1, N_MOD, d), lambda i: (i // tiles_per_batch, 0, 0)),
            _resident(w_out.shape),
            _resident((1, d)),
            _resident(w1.shape), _resident(w3.shape), _resident(w2.shape),
            _resident((1, d)),
        ],
        out_specs=pl.BlockSpec((tm, d), lambda i: (i, 0)),
        out_shape=jax.ShapeDtypeStruct((t, d), F32),
        compiler_params=_params(),
        name="out_ffn2",
    )(x1, ya, yb, mod, w_out, norm_g, w1, w3, w2, final_g)


def _swap_halves(w, axis):
    lo, hi = jnp.split(w, 2, axis=axis)
    return jnp.concatenate([hi, lo], axis=axis)


def kernel(x, c, positions, ada_w, ada_b, norm_ffn1_g, ffn1_w1, ffn1_w3, ffn1_w2, norm_mix_g, w_in, conv_w, q_norm_g, w_uq, kv_norm_g, w_ukv, out_norm_g, w_out, norm_ffn2_g, ffn2_w1, ffn2_w3, ffn2_w2, final_norm_g):
    bsz, seq, d = x.shape
    depth = ada_w.shape[0]
    t = bsz * seq
    tiles_per_batch = seq // ROW_TILE
    half = QK_ROPE // 2

    assert depth == 1, "only the one-layer block is implemented"
    l = 0

    cs_t, cs_row, (f1w1, f1w3, f1w2) = _rope_tables(positions, [ffn1_w1[l], ffn1_w3[l], ffn1_w2[l]])

    xf = x.reshape(t, d)
    mod = _adaln_mod(c, ada_w[l], ada_b[l]).reshape(bsz, N_MOD, d)

    wq = w_uq[l].T.reshape(MLA_HEADS, QK_NOPE + QK_ROPE, Q_LORA)
    wq_rope = wq[:, QK_NOPE:]
    wq_t = jnp.concatenate([wq, _swap_halves(wq_rope, 1)], axis=1).reshape(
        MLA_HEADS * QK_PAD, Q_LORA).astype(BF16)
    wkv = w_ukv[l].reshape(KV_LORA, MLA_HEADS, QK_NOPE + V_HEAD)
    wk = wkv[:, :, :QK_NOPE].reshape(KV_LORA, MLA_HEADS * QK_NOPE).astype(BF16)
    wv_t = wkv[:, :, QK_NOPE:].reshape(KV_LORA, MLA_WIDTH).T.astype(BF16)

    x1, w_main, w_tail, (wo, f2w1, f2w3, f2w2) = _ffn1(
        xf, mod, norm_ffn1_g[l].reshape(1, d), f1w1, f1w3, f1w2, tiles_per_batch,
        w_in[l], [w_out[l], ffn2_w1[l], ffn2_w3[l], ffn2_w2[l]])
    ya, q_t, k, v_t = _proj(
        x1, mod, norm_mix_g[l].reshape(1, d), w_main, w_tail, conv_w[l],
        out_norm_g[l, :CONV_WIDTH].reshape(1, CONV_WIDTH), q_norm_g[l].reshape(1, Q_LORA), wq_t,
        kv_norm_g[l].reshape(1, KV_LORA), wk, wv_t, cs_row, cs_t, tiles_per_batch)
    yb = _attention(q_t, k, v_t, out_norm_g[l, CONV_WIDTH:].reshape(1, MLA_WIDTH), bsz, seq)
    xf = _out_ffn2(x1, ya, yb, mod, wo, norm_ffn2_g[l].reshape(1, d), f2w1, f2w3, f2w2,
                   final_norm_g.reshape(1, d), tiles_per_batch)
    return xf.reshape(bsz, seq, d)
```

```python
import functools
import math

import jax
import jax.numpy as jnp
from jax import lax
from jax.experimental import pallas as pl
from jax.experimental.pallas import tpu as pltpu

CHUNK = 64
EPS = 1e-6
N_MOD = 9
CONV_WIDTH = 512
CONV_GROUPS = 8
CONV_K = 3
MLA_HEADS = 4
QK_NOPE = 128
QK_ROPE = 64
V_HEAD = 128
Q_LORA = 384
KV_LORA = 256
ROPE_THETA = 10000.0
MLA_WIDTH = MLA_HEADS * V_HEAD

LANES = 128
SUBLANES = 8
MXU_TILE = 256
QK_PAD = MXU_TILE
BF16_ROWS = 16
V_ROWS = V_HEAD + BF16_ROWS
VMEM_LIMIT_BYTES = 56 * 1024 * 1024

ROW_TILE = 512
ATTN_TQ = 512
ATTN_UNROLL = 4
MOD_TN = 1024
ROPE_TC = 2048

F32 = jnp.float32
BF16 = jnp.bfloat16


def _params(n_axes=1, flags=None):
    return pltpu.CompilerParams(
        dimension_semantics=("arbitrary",) * n_axes,
        vmem_limit_bytes=VMEM_LIMIT_BYTES,
        flags=flags,
    )


def _resident(shape):
    nd = len(shape)
    return pl.BlockSpec(shape, lambda *_: (0,) * nd, pipeline_mode=pl.Buffered(1))


def _rms_scale(x):
    return lax.rsqrt(jnp.mean(x * x, axis=-1, keepdims=True) + EPS)


def _mod_kernel(cb_ref, w_ref, b_ref, o_ref):
    w = w_ref[...]
    tn = w.shape[1]
    for b in range(cb_ref.shape[0]):
        cb = cb_ref[b]
        s = cb * jax.nn.sigmoid(cb)
        cols = [jnp.sum(w[:, j * LANES:(j + 1) * LANES] * s, axis=0, keepdims=True)
                for j in range(tn // LANES)]
        o_ref[b:b + 1, :] = jnp.concatenate(cols, axis=1) + b_ref[...]


def _adaln_mod(c, ada_w, ada_b):
    bsz, d = c.shape
    n = ada_w.shape[1]
    cb = jnp.broadcast_to(c[:, :, None], (bsz, d, LANES))
    return pl.pallas_call(
        _mod_kernel,
        grid=(n // MOD_TN,),
        in_specs=[
            pl.BlockSpec((bsz, d, LANES), lambda j: (0, 0, 0)),
            pl.BlockSpec((d, MOD_TN), lambda j: (0, j)),
            pl.BlockSpec((1, MOD_TN), lambda j: (0, j)),
        ],
        out_specs=pl.BlockSpec((bsz, MOD_TN), lambda j: (0, j)),
        out_shape=jax.ShapeDtypeStruct((bsz, n), F32),
        compiler_params=_params(),
        name="adaln_mod",
    )(cb, ada_w, ada_b.reshape(1, n))


def _cast_streams(weights, n_steps):
    in_specs, out_specs, out_shapes = [], [], []
    for w in weights:
        rows, width = w.shape
        s = max(k for k in range(1, n_steps + 1)
                if rows % k == 0 and (rows // k) % BF16_ROWS == 0)
        spec = pl.BlockSpec((rows // s, width), lambda i, s=s: (jnp.minimum(i, s - 1), 0))
        in_specs.append(spec)
        out_specs.append(spec)
        out_shapes.append(jax.ShapeDtypeStruct((rows, width), BF16))
    return in_specs, out_specs, out_shapes


def _run_casts(src_refs, dst_refs):
    for src, dst in zip(src_refs, dst_refs):
        dst[...] = src[...].astype(dst.dtype)


def _rope_kernel(pos_ref, inv_ref, *refs):
    n_cast = (len(refs) - 2) // 2
    cst_ref, csr_ref = refs[n_cast], refs[n_cast + 1]
    ang = pos_ref[...].astype(F32) * inv_ref[...]
    c = jnp.cos(ang)
    s = jnp.sin(ang)
    cst = jnp.concatenate([c, c, -s, s], axis=0)
    cst_ref[...] = cst
    csr_ref[...] = cst.T
    _run_casts(refs[:n_cast], refs[n_cast + 2:])


def _rope_tables(positions, cast_weights):
    half = QK_ROPE // 2
    t = positions.size
    tc = ROPE_TC
    n_steps = t // tc
    inv_freq = ROPE_THETA ** (-jnp.arange(0, QK_ROPE, 2, dtype=F32) / QK_ROPE)
    c_in, c_out, c_shapes = _cast_streams(cast_weights, n_steps)
    outs = pl.pallas_call(
        _rope_kernel,
        grid=(n_steps,),
        in_specs=[pl.BlockSpec((1, tc), lambda i: (0, i)),
                  pl.BlockSpec((half, 1), lambda i: (0, 0))] + c_in,
        out_specs=[pl.BlockSpec((4 * half, tc), lambda i: (0, i)),
                   pl.BlockSpec((tc, 4 * half), lambda i: (i, 0))] + c_out,
        out_shape=[jax.ShapeDtypeStruct((4 * half, t), F32),
                   jax.ShapeDtypeStruct((t, 4 * half), F32)] + c_shapes,
        compiler_params=_params(),
        name="rope_tables",
    )(positions.reshape(1, t), inv_freq.reshape(half, 1), *cast_weights)
    return outs[0], outs[1], outs[2:]


def _ffn_update(x, shift, scale, gate, norm_g, w1_ref, w3_ref, w2_ref):
    h = (x * _rms_scale(x) * (norm_g * (1.0 + scale)) + shift).astype(BF16)
    a = jnp.dot(h, w1_ref[...], preferred_element_type=F32)
    b = jnp.dot(h, w3_ref[...], preferred_element_type=F32)
    g = (a * jax.nn.sigmoid(a) * b).astype(BF16)
    o = jnp.dot(g, w2_ref[...], preferred_element_type=F32)
    return x + (0.5 * gate) * o


def _ffn1_kernel(x_ref, mod_ref, ng_ref, w1_ref, w3_ref, w2_ref, win_ref, *refs):
    n_cast = (len(refs) - 3) // 2
    o_ref, wmain_ref, wtail_ref = refs[n_cast:n_cast + 3]
    mod = mod_ref[0]
    o_ref[...] = _ffn_update(x_ref[...], mod[0:1], mod[1:2], mod[2:3], ng_ref[...],
                             w1_ref, w3_ref, w2_ref)
    _run_casts(refs[:n_cast], refs[n_cast + 3:])
    w = win_ref[...]
    n_main = wmain_ref.shape[1]
    kr0 = w.shape[1] - QK_ROPE
    half = QK_ROPE // 2
    wmain_ref[...] = w[:, :n_main].astype(BF16)
    wtail_ref[...] = jnp.concatenate(
        [w[:, n_main:], w[:, kr0 + half:], w[:, kr0:kr0 + half]], axis=1).astype(BF16)


def _ffn1(x2d, mod, norm_g, w1, w3, w2, tiles_per_batch, w_in, cast_weights):
    t, d = x2d.shape
    tm = ROW_TILE
    n_steps = t // tm
    c_in, c_out, c_shapes = _cast_streams(cast_weights, n_steps)
    rows, in_cols = w_in.shape
    n_main = (in_cols // MXU_TILE) * MXU_TILE
    assert in_cols - n_main + QK_ROPE == MXU_TILE and rows % n_steps == 0
    wr = rows // n_steps
    outs = pl.pallas_call(
        _ffn1_kernel,
        grid=(n_steps,),
        in_specs=[
            pl.BlockSpec((tm, d), lambda i: (i, 0)),
            pl.BlockSpec((1, N_MOD, d), lambda i: (i // tiles_per_batch, 0, 0)),
            _resident((1, d)),
            _resident(w1.shape), _resident(w3.shape), _resident(w2.shape),
            pl.BlockSpec((wr, in_cols), lambda i: (i, 0)),
        ] + c_in,
        out_specs=[pl.BlockSpec((tm, d), lambda i: (i, 0)),
                   pl.BlockSpec((wr, n_main), lambda i: (i, 0)),
                   pl.BlockSpec((wr, MXU_TILE), lambda i: (i, 0))] + c_out,
        out_shape=[jax.ShapeDtypeStruct((t, d), F32),
                   jax.ShapeDtypeStruct((rows, n_main), BF16),
                   jax.ShapeDtypeStruct((rows, MXU_TILE), BF16)] + c_shapes,
        compiler_params=_params(),
        name="ffn1",
    )(x2d, mod, norm_g, w1, w3, w2, w_in, *cast_weights)
    return outs[0], outs[1], outs[2], outs[3:]


def _proj_kernel(x_ref, mod_ref, ng_ref, win_ref, wtail_ref, cw_ref, gna_ref, qng_ref, wq_ref, kvng_ref,
                 wk_ref, wv_ref, cs_ref, cst_ref,
                 ya_ref, q_ref, k_ref, v_ref, zbuf_a, zbuf_b, ubuf, *, tiles_per_batch):
    i = pl.program_id(0)

    @pl.when(i == 0)
    def _():
        zbuf_b[...] = jnp.zeros(zbuf_b.shape, F32)
        ubuf[0:SUBLANES, :] = jnp.zeros((SUBLANES, CONV_WIDTH), F32)

    def step(z_new, z_old):
        x = x_ref[...]
        mod = mod_ref[0]
        h = (x * _rms_scale(x) * (ng_ref[...] * (1.0 + mod[4:5])) + mod[3:4]).astype(BF16)
        post = _mixer_post_pieces(z_old, i - 1, cw_ref, gna_ref, qng_ref, wq_ref, kvng_ref, wk_ref,
                                  wv_ref, cs_ref, cst_ref, ya_ref, q_ref, k_ref, v_ref, ubuf,
                                  tiles_per_batch)
        n_main = win_ref.shape[1] // MXU_TILE
        assert wtail_ref.shape[1] == MXU_TILE and len(post) == n_main + 1
        for c in range(n_main + 1):
            cols = slice(c * MXU_TILE, (c + 1) * MXU_TILE)
            w_c = win_ref[:, cols] if c < n_main else wtail_ref[...]
            z_new[:, cols] = jnp.dot(h, w_c, preferred_element_type=F32)
            post[c]()

    pl.when(i % 2 == 0)(functools.partial(step, zbuf_a, zbuf_b))
    pl.when(i % 2 == 1)(functools.partial(step, zbuf_b, zbuf_a))


def _mixer_post_pieces(z, tile, cw_ref, gna_ref, qng_ref, wq_ref, kvng_ref, wk_ref, wv_ref, cs_ref,
                       cst_ref, ya_ref, q_ref, k_ref, v_ref, ubuf, tiles_per_batch):
    tm = z.shape[0]
    cwid = CONV_WIDTH
    c0 = 3 * cwid
    nt = (((1,), (1,)), ((), ()))
    scale = (QK_NOPE + QK_ROPE) ** -0.5 * math.log2(math.e)
    state = {}

    def q_head(hh):
        if hh == 0:
            cq = z[:, c0:c0 + Q_LORA]
            state["cqn"] = (cq * _rms_scale(cq) * qng_ref[...]).astype(BF16)
        base = hh * QK_PAD
        qt = lax.dot_general(wq_ref[base:base + QK_PAD, :], state["cqn"], nt,
                             preferred_element_type=F32)
        ct = cst_ref[0:QK_ROPE, :]
        st = cst_ref[QK_ROPE:2 * QK_ROPE, :]
        rope = qt[QK_NOPE:QK_NOPE + QK_ROPE] * ct + qt[QK_NOPE + QK_ROPE:QK_PAD] * st
        q_ref[0, base:base + QK_PAD, :] = jnp.concatenate(
            [qt[0:QK_NOPE] * scale, rope * scale, jnp.zeros((QK_ROPE, tm), F32)],
            axis=0).astype(q_ref.dtype)

    def kv():
        ckv = z[:, c0 + Q_LORA:c0 + Q_LORA + KV_LORA]
        ckvn = (ckv * _rms_scale(ckv) * kvng_ref[...]).astype(BF16)
        kn = jnp.dot(ckvn, wk_ref[...], preferred_element_type=F32)
        krr = z[:, c0 + Q_LORA + KV_LORA:c0 + Q_LORA + KV_LORA + LANES]
        a = krr * cs_ref[...]
        kr = a + pltpu.roll(a, QK_ROPE, axis=1)
        kp = []
        for hh in range(MLA_HEADS):
            kp += [kn[:, hh * QK_NOPE:(hh + 1) * QK_NOPE], kr]
        k_ref[...] = jnp.concatenate(kp, axis=1).astype(k_ref.dtype)
        vt = lax.dot_general(wv_ref[...], ckvn, nt, preferred_element_type=F32)
        ones = jnp.ones((V_ROWS - V_HEAD, tm), F32)
        vp = []
        for hh in range(MLA_HEADS):
            vp += [vt[hh * V_HEAD:(hh + 1) * V_HEAD], ones]
        v_ref[0] = jnp.concatenate(vp, axis=0).astype(v_ref.dtype)

    def conv_tile(j):
        cols = slice(j * LANES, (j + 1) * LANES)
        ubuf[0:SUBLANES, cols] = jnp.where(tile % tiles_per_batch == 0, 0.0, ubuf[0:SUBLANES, cols])
        u = z[:, cwid + j * LANES:cwid + (j + 1) * LANES] * z[:, 2 * cwid + j * LANES:2 * cwid + (j + 1) * LANES]
        ubuf[SUBLANES:SUBLANES + tm, cols] = u
        u1 = ubuf[SUBLANES - 1:SUBLANES - 1 + tm, cols]
        u2 = ubuf[SUBLANES - 2:SUBLANES - 2 + tm, cols]
        cw = cw_ref[:, cols]
        ya = z[:, cols] * (cw[0:1] * u2 + cw[1:2] * u1 + cw[2:3] * u)
        ubuf[0:SUBLANES, cols] = ubuf[tm:tm + SUBLANES, cols]
        gw = cwid // CONV_GROUPS
        low = lax.broadcasted_iota(jnp.int32, (tm, LANES), 1) < gw
        sq = ya * ya
        s_lo = jnp.sum(jnp.where(low, sq, 0.0), axis=-1, keepdims=True)
        s_hi = jnp.sum(jnp.where(low, 0.0, sq), axis=-1, keepdims=True)
        ms = jnp.where(low, s_lo, s_hi) * (1.0 / gw)
        ya_ref[:, cols] = (ya * lax.rsqrt(ms + EPS) * gna_ref[:, cols]).astype(ya_ref.dtype)

    pieces = [functools.partial(q_head, hh) for hh in range(MLA_HEADS)] + [kv]
    pieces += [functools.partial(conv_tile, j) for j in range(cwid // LANES)]
    return pieces


def _proj(x1, mod, norm_g, w_main, w_tail, conv_w, gn_a, q_norm_g, wq_t, kv_norm_g, wk, wv_t, cs_row, cs_t,
          tiles_per_batch):
    t, d = x1.shape
    tm = ROW_TILE
    nt = t // tm
    hq = MLA_HEADS * QK_PAD

    def cur(i):
        return jnp.minimum(i, nt - 1)

    def prev(i):
        return jnp.maximum(i - 1, 0)

    return pl.pallas_call(
        functools.partial(_proj_kernel, tiles_per_batch=tiles_per_batch),
        grid=(nt + 1,),
        in_specs=[
            pl.BlockSpec((tm, d), lambda i: (cur(i), 0)),
            pl.BlockSpec((1, N_MOD, d), lambda i: (cur(i) // tiles_per_batch, 0, 0)),
            _resident((1, d)),
            _resident(w_main.shape),
            _resident(w_tail.shape),
            _resident(conv_w.shape),
            _resident(gn_a.shape),
            _resident(q_norm_g.shape),
            _resident(wq_t.shape),
            _resident(kv_norm_g.shape),
            _resident(wk.shape),
            _resident(wv_t.shape),
            pl.BlockSpec((tm, LANES), lambda i: (prev(i), 0)),
            pl.BlockSpec((LANES, tm), lambda i: (0, prev(i))),
        ],
        out_specs=[
            pl.BlockSpec((tm, CONV_WIDTH), lambda i: (prev(i), 0)),
            pl.BlockSpec((1, hq, tm), lambda i: (prev(i), 0, 0)),
            pl.BlockSpec((tm, hq), lambda i: (prev(i), 0)),
            pl.BlockSpec((1, MLA_HEADS * V_ROWS, tm), lambda i: (prev(i), 0, 0)),
        ],
        out_shape=[
            jax.ShapeDtypeStruct((t, CONV_WIDTH), BF16),
            jax.ShapeDtypeStruct((nt, hq, tm), BF16),
            jax.ShapeDtypeStruct((t, hq), BF16),
            jax.ShapeDtypeStruct((nt, MLA_HEADS * V_ROWS, tm), BF16),
        ],
        scratch_shapes=[pltpu.VMEM((tm, w_main.shape[1] + w_tail.shape[1]), F32),
                        pltpu.VMEM((tm, w_main.shape[1] + w_tail.shape[1]), F32),
                        pltpu.VMEM((tm + SUBLANES, CONV_WIDTH), F32)],
        compiler_params=_params(),
        name="mixer_proj",
    )(x1, mod, norm_g, w_main, w_tail, conv_w, gn_a, q_norm_g, wq_t, kv_norm_g, wk, wv_t, cs_row, cs_t)


def _attn_kernel(q_ref, k_ref, v_ref, g_ref, o_ref, s_sc, bm_sc, m_sc, acc_sc):
    i = pl.program_id(1)
    tq = q_ref.shape[2]
    tk = v_ref.shape[2]
    m_sc[...] = jnp.full(m_sc.shape, -jnp.inf, F32)
    acc_sc[...] = jnp.zeros(acc_sc.shape, F32)

    def chunk_mask(s):
        kc = lax.broadcasted_iota(jnp.int32, (tk, tq), 0) // CHUNK
        qc = lax.broadcasted_iota(jnp.int32, (tk, tq), 1) // CHUNK
        return jnp.where(kc <= qc, s, -1e30)

    def scores(j, hh, slot, masked):
        row0 = pl.multiple_of(j * tk, tk)
        s = jnp.dot(k_ref[pl.ds(row0, tk), hh * QK_PAD:(hh + 1) * QK_PAD],
                    q_ref[0, hh * QK_PAD:(hh + 1) * QK_PAD, :],
                    preferred_element_type=F32)
        if masked:
            s = chunk_mask(s)
        s_sc[slot] = s
        bm_sc[slot] = jnp.max(s, axis=0, keepdims=True)

    def softmax_pv(j, hh, slot, remask):
        s = s_sc[slot]
        if remask:
            s = chunk_mask(s)
            block_max = jnp.max(s, axis=0, keepdims=True)
        else:
            block_max = bm_sc[slot]
        m_prev = m_sc[hh]
        m_new = jnp.maximum(m_prev, block_max)
        alpha = jnp.exp2(m_prev - m_new)
        p = jnp.exp2(s - m_new).astype(BF16)
        pv = jnp.dot(v_ref[j, hh * V_ROWS:(hh + 1) * V_ROWS, :], p,
                     preferred_element_type=F32)
        acc_sc[hh] = alpha * acc_sc[hh] + pv
        m_sc[hh] = m_new

    scores(0, 0, 0, False)

    def block(j, masked, next_j):
        for hh in range(MLA_HEADS):
            if hh + 1 < MLA_HEADS:
                scores(j, hh + 1, (hh + 1) % 2, masked)
            elif next_j is not None:
                scores(next_j, 0, 0, False)
            softmax_pv(j, hh, hh % 2, remask=masked and hh == 0)

    def run(j0, n):
        for t in range(n):
            block(j0 + t, False, j0 + t + 1)

    n_quads = lax.shift_right_logical(i, 2)

    def quad(q, carry):
        run(q * ATTN_UNROLL, ATTN_UNROLL)
        return carry

    lax.fori_loop(0, n_quads, quad, 0)
    done = n_quads * ATTN_UNROLL

    @pl.when((i & 2) != 0)
    def _():
        run(done, 2)

    @pl.when((i & 1) != 0)
    def _():
        run(done + (i & 2), 1)

    block(i, True, None)

    g = g_ref[...]
    for hh in range(MLA_HEADS):
        acc = acc_sc[hh]
        o_t = acc[0:V_HEAD] / acc[V_HEAD:V_HEAD + 1]
        o_t = o_t * lax.rsqrt(jnp.mean(o_t * o_t, axis=0, keepdims=True) + EPS)
        o_ref[:, hh * V_HEAD:(hh + 1) * V_HEAD] = (
            o_t.T * g[:, hh * V_HEAD:(hh + 1) * V_HEAD]).astype(o_ref.dtype)


def _attention(q_t, k, v_t, gn_b, bsz, seq):
    tq = ATTN_TQ
    tk = v_t.shape[2]
    assert tq == tk
    nq = seq // tq
    hq = MLA_HEADS * QK_PAD
    return pl.pallas_call(
        _attn_kernel,
        grid=(bsz, nq),
        in_specs=[
            pl.BlockSpec((1, hq, tq), lambda b, i: (b * nq + i, 0, 0)),
            pl.BlockSpec((seq, hq), lambda b, i: (b, 0), pipeline_mode=pl.Buffered(1)),
            pl.BlockSpec((seq // tk, MLA_HEADS * V_ROWS, tk), lambda b, i: (b, 0, 0),
                         pipeline_mode=pl.Buffered(1)),
            _resident(gn_b.shape),
        ],
        out_specs=pl.BlockSpec((tq, MLA_WIDTH), lambda b, i: (b * nq + i, 0)),
        out_shape=jax.ShapeDtypeStruct((bsz * seq, MLA_WIDTH), BF16),
        scratch_shapes=[
            pltpu.VMEM((2, tk, tq), F32),
            pltpu.VMEM((2, 1, tq), F32),
            pltpu.VMEM((MLA_HEADS, 1, tq), F32),
            pltpu.VMEM((MLA_HEADS, V_ROWS, tq), F32),
        ],
        compiler_params=_params(2),
        name="mla_attention",
    )(q_t, k, v_t, gn_b)


def _out_kernel(x_ref, ya_ref, yb_ref, mod_ref, wo_ref, ng_ref, w1_ref, w3_ref, w2_ref, fg_ref, o_ref):
    mod = mod_ref[0]
    y = jnp.concatenate([ya_ref[...], yb_ref[...]], axis=1)
    x = x_ref[...] + mod[5:6] * jnp.dot(y, wo_ref[...], preferred_element_type=F32)
    x = _ffn_update(x, mod[6:7], mod[7:8], mod[8:9], ng_ref[...], w1_ref, w3_ref, w2_ref)
    o_ref[...] = x * _rms_scale(x) * fg_ref[...]


def _out_ffn2(x1, ya, yb, mod, w_out, norm_g, w1, w3, w2, final_g, tiles_per_batch):
    t, d = x1.shape
    tm = ROW_TILE
    return pl.pallas_call(
        _out_kernel,
        grid=(t // tm,),
        in_specs=[
            pl.BlockSpec((tm, d), lambda i: (i, 0)),
            pl.BlockSpec((tm, CONV_WIDTH), lambda i: (i, 0)),
            pl.BlockSpec((tm, MLA_WIDTH), lambda i: (i, 0)),
            pl.BlockSpec((1, N_MOD, d), lambda i: (i // tiles_per_batch, 0, 0)),
            _resident(w_out.shape),
            _resident((1, d)),
            _resident(w1.shape), _resident(w3.shape), _resident(w2.shape),
            _resident((1, d)),
        ],
        out_specs=pl.BlockSpec((tm, d), lambda i: (i, 0)),
        out_shape=jax.ShapeDtypeStruct((t, d), F32),
        compiler_params=_params(),
        name="out_ffn2",
    )(x1, ya, yb, mod, w_out, norm_g, w1, w3, w2, final_g)


def _swap_halves(w, axis):
    lo, hi = jnp.split(w, 2, axis=axis)
    return jnp.concatenate([hi, lo], axis=axis)


def kernel(x, c, positions, ada_w, ada_b, norm_ffn1_g, ffn1_w1, ffn1_w3, ffn1_w2, norm_mix_g, w_in, conv_w, q_norm_g, w_uq, kv_norm_g, w_ukv, out_norm_g, w_out, norm_ffn2_g, ffn2_w1, ffn2_w3, ffn2_w2, final_norm_g):
    bsz, seq, d = x.shape
    depth = ada_w.shape[0]
    t = bsz * seq
    tiles_per_batch = seq // ROW_TILE
    half = QK_ROPE // 2

    assert depth == 1, "only the one-layer block is implemented"
    l = 0

    cs_t, cs_row, (f1w1, f1w3, f1w2) = _rope_tables(positions, [ffn1_w1[l], ffn1_w3[l], ffn1_w2[l]])

    xf = x.reshape(t, d)
    mod = _adaln_mod(c, ada_w[l], ada_b[l]).reshape(bsz, N_MOD, d)

    wq = w_uq[l].T.reshape(MLA_HEADS, QK_NOPE + QK_ROPE, Q_LORA)
    wq_rope = wq[:, QK_NOPE:]
    wq_t = jnp.concatenate([wq, _swap_halves(wq_rope, 1)], axis=1).reshape(
        MLA_HEADS * QK_PAD, Q_LORA).astype(BF16)
    wkv = w_ukv[l].reshape(KV_LORA, MLA_HEADS, QK_NOPE + V_HEAD)
    wk = wkv[:, :, :QK_NOPE].reshape(KV_LORA, MLA_HEADS * QK_NOPE).astype(BF16)
    wv_t = wkv[:, :, QK_NOPE:].reshape(KV_LORA, MLA_WIDTH).T.astype(BF16)

    x1, w_main, w_tail, (wo, f2w1, f2w3, f2w2) = _ffn1(
        xf, mod, norm_ffn1_g[l].reshape(1, d), f1w1, f1w3, f1w2, tiles_per_batch,
        w_in[l], [w_out[l], ffn2_w1[l], ffn2_w3[l], ffn2_w2[l]])
    ya, q_t, k, v_t = _proj(
        x1, mod, norm_mix_g[l].reshape(1, d), w_main, w_tail, conv_w[l],
        out_norm_g[l, :CONV_WIDTH].reshape(1, CONV_WIDTH), q_norm_g[l].reshape(1, Q_LORA), wq_t,
        kv_norm_g[l].reshape(1, KV_LORA), wk, wv_t, cs_row, cs_t, tiles_per_batch)
    yb = _attention(q_t, k, v_t, out_norm_g[l, CONV_WIDTH:].reshape(1, MLA_WIDTH), bsz, seq)
    xf = _out_ffn2(x1, ya, yb, mod, wo, norm_ffn2_g[l].reshape(1, d), f2w1, f2w3, f2w2,
                   final_norm_g.reshape(1, d), tiles_per_batch)
    return xf.reshape(bsz, seq, d)
```

```python
import functools
import math

import jax
import jax.numpy as jnp
from jax import lax
from jax.experimental import pallas as pl
from jax.experimental.pallas import tpu as pltpu

CHUNK = 64
EPS = 1e-6
N_MOD = 9
CONV_WIDTH = 512
CONV_GROUPS = 8
CONV_K = 3
MLA_HEADS = 4
QK_NOPE = 128
QK_ROPE = 64
V_HEAD = 128
Q_LORA = 384
KV_LORA = 256
ROPE_THETA = 10000.0
MLA_WIDTH = MLA_HEADS * V_HEAD

LANES = 128
SUBLANES = 8
MXU_TILE = 256
QK_PAD = MXU_TILE
BF16_ROWS = 16
V_ROWS = V_HEAD + BF16_ROWS
VMEM_LIMIT_BYTES = 56 * 1024 * 1024

ROW_TILE = 512
ATTN_TQ = 512
ATTN_UNROLL = 4
PROLOGUE_STEPS = 8

F32 = jnp.float32
BF16 = jnp.bfloat16


def _params(n_axes=1, flags=None):
    return pltpu.CompilerParams(
        dimension_semantics=("arbitrary",) * n_axes,
        vmem_limit_bytes=VMEM_LIMIT_BYTES,
        flags=flags,
    )


def _resident(shape):
    nd = len(shape)
    return pl.BlockSpec(shape, lambda *_: (0,) * nd, pipeline_mode=pl.Buffered(1))


def _rms_scale(x):
    return lax.rsqrt(jnp.mean(x * x, axis=-1, keepdims=True) + EPS)


def _cast_streams(weights, n_steps):
    in_specs, out_specs, out_shapes = [], [], []
    for w in weights:
        rows, width = w.shape
        s = max(k for k in range(1, n_steps + 1)
                if rows % k == 0 and (rows // k) % BF16_ROWS == 0)
        spec = pl.BlockSpec((rows // s, width), lambda i, s=s: (jnp.minimum(i, s - 1), 0))
        in_specs.append(spec)
        out_specs.append(spec)
        out_shapes.append(jax.ShapeDtypeStruct((rows, width), BF16))
    return in_specs, out_specs, out_shapes


def _run_casts(src_refs, dst_refs):
    for src, dst in zip(src_refs, dst_refs):
        dst[...] = src[...].astype(dst.dtype)


def _prologue_kernel(pos_ref, inv_ref, cb_ref, w_ref, b_ref, *refs):
    n_cast = (len(refs) - 3) // 2
    cst_ref, csr_ref, mod_ref = refs[n_cast:n_cast + 3]

    ang = pos_ref[...].astype(F32) * inv_ref[...]
    c = jnp.cos(ang)
    s = jnp.sin(ang)
    cst = jnp.concatenate([c, c, -s, s], axis=0)
    cst_ref[...] = cst
    csr_ref[...] = cst.T

    w = w_ref[...]
    tn = w.shape[1]
    for b in range(cb_ref.shape[0]):
        cb = cb_ref[b]
        act = cb * jax.nn.sigmoid(cb)
        cols = [jnp.sum(w[:, j * LANES:(j + 1) * LANES] * act, axis=0, keepdims=True)
                for j in range(tn // LANES)]
        mod_ref[b:b + 1, :] = jnp.concatenate(cols, axis=1) + b_ref[...]

    _run_casts(refs[:n_cast], refs[n_cast + 3:])


def _prologue(positions, c, ada_w, ada_b, cast_weights):
    half = QK_ROPE // 2
    t = positions.size
    n_steps = PROLOGUE_STEPS
    tc = t // n_steps
    bsz, d = c.shape
    n = ada_w.shape[1]
    tn = n // n_steps
    assert t % n_steps == 0 and n % n_steps == 0 and tn % LANES == 0 and tc % LANES == 0
    inv_freq = ROPE_THETA ** (-jnp.arange(0, QK_ROPE, 2, dtype=F32) / QK_ROPE)
    cb = jnp.broadcast_to(c[:, :, None], (bsz, d, LANES))
    c_in, c_out, c_shapes = _cast_streams(cast_weights, n_steps)
    outs = pl.pallas_call(
        _prologue_kernel,
        grid=(n_steps,),
        in_specs=[pl.BlockSpec((1, tc), lambda i: (0, i)),
                  pl.BlockSpec((half, 1), lambda i: (0, 0)),
                  pl.BlockSpec((bsz, d, LANES), lambda i: (0, 0, 0)),
                  pl.BlockSpec((d, tn), lambda i: (0, i)),
                  pl.BlockSpec((1, tn), lambda i: (0, i))] + c_in,
        out_specs=[pl.BlockSpec((4 * half, tc), lambda i: (0, i)),
                   pl.BlockSpec((tc, 4 * half), lambda i: (i, 0)),
                   pl.BlockSpec((bsz, tn), lambda i: (0, i))] + c_out,
        out_shape=[jax.ShapeDtypeStruct((4 * half, t), F32),
                   jax.ShapeDtypeStruct((t, 4 * half), F32),
                   jax.ShapeDtypeStruct((bsz, n), F32)] + c_shapes,
        compiler_params=_params(),
        name="prologue",
    )(positions.reshape(1, t), inv_freq.reshape(half, 1), cb, ada_w, ada_b.reshape(1, n), *cast_weights)
    return outs[0], outs[1], outs[2], outs[3:]


def _ffn_update(x, shift, scale, gate, norm_g, w1_ref, w3_ref, w2_ref):
    h = (x * _rms_scale(x) * (norm_g * (1.0 + scale)) + shift).astype(BF16)
    a = jnp.dot(h, w1_ref[...], preferred_element_type=F32)
    b = jnp.dot(h, w3_ref[...], preferred_element_type=F32)
    g = (a * jax.nn.sigmoid(a) * b).astype(BF16)
    o = jnp.dot(g, w2_ref[...], preferred_element_type=F32)
    return x + (0.5 * gate) * o


def _ffn1_kernel(x_ref, mod_ref, ng_ref, w1_ref, w3_ref, w2_ref, win_ref, *refs):
    n_cast = (len(refs) - 3) // 2
    o_ref, wmain_ref, wtail_ref = refs[n_cast:n_cast + 3]
    mod = mod_ref[0]
    o_ref[...] = _ffn_update(x_ref[...], mod[0:1], mod[1:2], mod[2:3], ng_ref[...],
                             w1_ref, w3_ref, w2_ref)
    _run_casts(refs[:n_cast], refs[n_cast + 3:])
    w = win_ref[...]
    n_main = wmain_ref.shape[1]
    kr0 = w.shape[1] - QK_ROPE
    half = QK_ROPE // 2
    wmain_ref[...] = w[:, :n_main].astype(BF16)
    wtail_ref[...] = jnp.concatenate(
        [w[:, n_main:], w[:, kr0 + half:], w[:, kr0:kr0 + half]], axis=1).astype(BF16)


def _ffn1(x2d, mod, norm_g, w1, w3, w2, tiles_per_batch, w_in, cast_weights):
    t, d = x2d.shape
    tm = ROW_TILE
    n_steps = t // tm
    c_in, c_out, c_shapes = _cast_streams(cast_weights, n_steps)
    rows, in_cols = w_in.shape
    n_main = (in_cols // MXU_TILE) * MXU_TILE
    assert in_cols - n_main + QK_ROPE == MXU_TILE and rows % n_steps == 0
    wr = rows // n_steps
    outs = pl.pallas_call(
        _ffn1_kernel,
        grid=(n_steps,),
        in_specs=[
            pl.BlockSpec((tm, d), lambda i: (i, 0)),
            pl.BlockSpec((1, N_MOD, d), lambda i: (i // tiles_per_batch, 0, 0)),
            _resident((1, d)),
            _resident(w1.shape), _resident(w3.shape), _resident(w2.shape),
            pl.BlockSpec((wr, in_cols), lambda i: (i, 0)),
        ] + c_in,
        out_specs=[pl.BlockSpec((tm, d), lambda i: (i, 0)),
                   pl.BlockSpec((wr, n_main), lambda i: (i, 0)),
                   pl.BlockSpec((wr, MXU_TILE), lambda i: (i, 0))] + c_out,
        out_shape=[jax.ShapeDtypeStruct((t, d), F32),
                   jax.ShapeDtypeStruct((rows, n_main), BF16),
                   jax.ShapeDtypeStruct((rows, MXU_TILE), BF16)] + c_shapes,
        compiler_params=_params(),
        name="ffn1",
    )(x2d, mod, norm_g, w1, w3, w2, w_in, *cast_weights)
    return outs[0], outs[1], outs[2], outs[3:]


def _proj_kernel(x_ref, mod_ref, ng_ref, win_ref, wtail_ref, cw_ref, gna_ref, qng_ref, wq_ref, kvng_ref,
                 wk_ref, wv_ref, cs_ref, cst_ref,
                 ya_ref, q_ref, k_ref, v_ref, zbuf_a, zbuf_b, ubuf, *, tiles_per_batch):
    i = pl.program_id(0)

    @pl.when(i == 0)
    def _():
        zbuf_b[...] = jnp.zeros(zbuf_b.shape, F32)
        ubuf[0:SUBLANES, :] = jnp.zeros((SUBLANES, CONV_WIDTH), F32)

    def step(z_new, z_old):
        x = x_ref[...]
        mod = mod_ref[0]
        h = (x * _rms_scale(x) * (ng_ref[...] * (1.0 + mod[4:5])) + mod[3:4]).astype(BF16)
        post = _mixer_post_pieces(z_old, i - 1, cw_ref, gna_ref, qng_ref, wq_ref, kvng_ref, wk_ref,
                                  wv_ref, cs_ref, cst_ref, ya_ref, q_ref, k_ref, v_ref, ubuf,
                                  tiles_per_batch)
        n_main = win_ref.shape[1] // MXU_TILE
        assert wtail_ref.shape[1] == MXU_TILE and len(post) == n_main + 1
        for c in range(n_main + 1):
            cols = slice(c * MXU_TILE, (c + 1) * MXU_TILE)
            w_c = win_ref[:, cols] if c < n_main else wtail_ref[...]
            z_new[:, cols] = jnp.dot(h, w_c, preferred_element_type=F32)
            post[c]()

    pl.when(i % 2 == 0)(functools.partial(step, zbuf_a, zbuf_b))
    pl.when(i % 2 == 1)(functools.partial(step, zbuf_b, zbuf_a))


def _mixer_post_pieces(z, tile, cw_ref, gna_ref, qng_ref, wq_ref, kvng_ref, wk_ref, wv_ref, cs_ref,
                       cst_ref, ya_ref, q_ref, k_ref, v_ref, ubuf, tiles_per_batch):
    tm = z.shape[0]
    cwid = CONV_WIDTH
    c0 = 3 * cwid
    nt = (((1,), (1,)), ((), ()))
    scale = (QK_NOPE + QK_ROPE) ** -0.5 * math.log2(math.e)
    state = {}

    def q_head(hh):
        if hh == 0:
            cq = z[:, c0:c0 + Q_LORA]
            state["cqn"] = (cq * _rms_scale(cq) * qng_ref[...]).astype(BF16)
        base = hh * QK_PAD
        qt = lax.dot_general(wq_ref[base:base + QK_PAD, :], state["cqn"], nt,
                             preferred_element_type=F32)
        ct = cst_ref[0:QK_ROPE, :]
        st = cst_ref[QK_ROPE:2 * QK_ROPE, :]
        rope = qt[QK_NOPE:QK_NOPE + QK_ROPE] * ct + qt[QK_NOPE + QK_ROPE:QK_PAD] * st
        q_ref[0, base:base + QK_PAD, :] = jnp.concatenate(
            [qt[0:QK_NOPE] * scale, rope * scale, jnp.zeros((QK_ROPE, tm), F32)],
            axis=0).astype(q_ref.dtype)

    def kv():
        ckv = z[:, c0 + Q_LORA:c0 + Q_LORA + KV_LORA]
        ckvn = (ckv * _rms_scale(ckv) * kvng_ref[...]).astype(BF16)
        kn = jnp.dot(ckvn, wk_ref[...], preferred_element_type=F32)
        krr = z[:, c0 + Q_LORA + KV_LORA:c0 + Q_LORA + KV_LORA + LANES]
        a = krr * cs_ref[...]
        kr = a + pltpu.roll(a, QK_ROPE, axis=1)
        kp = []
        for hh in range(MLA_HEADS):
            kp += [kn[:, hh * QK_NOPE:(hh + 1) * QK_NOPE], kr]
        k_ref[...] = jnp.concatenate(kp, axis=1).astype(k_ref.dtype)
        vt = lax.dot_general(wv_ref[...], ckvn, nt, preferred_element_type=F32)
        ones = jnp.ones((V_ROWS - V_HEAD, tm), F32)
        vp = []
        for hh in range(MLA_HEADS):
            vp += [vt[hh * V_HEAD:(hh + 1) * V_HEAD], ones]
        v_ref[0] = jnp.concatenate(vp, axis=0).astype(v_ref.dtype)

    def conv_tile(j):
        cols = slice(j * LANES, (j + 1) * LANES)
        ubuf[0:SUBLANES, cols] = jnp.where(tile % tiles_per_batch == 0, 0.0, ubuf[0:SUBLANES, cols])
        u = z[:, cwid + j * LANES:cwid + (j + 1) * LANES] * z[:, 2 * cwid + j * LANES:2 * cwid + (j + 1) * LANES]
        ubuf[SUBLANES:SUBLANES + tm, cols] = u
        u1 = ubuf[SUBLANES - 1:SUBLANES - 1 + tm, cols]
        u2 = ubuf[SUBLANES - 2:SUBLANES - 2 + tm, cols]
        cw = cw_ref[:, cols]
        ya = z[:, cols] * (cw[0:1] * u2 + cw[1:2] * u1 + cw[2:3] * u)
        ubuf[0:SUBLANES, cols] = ubuf[tm:tm + SUBLANES, cols]
        gw = cwid // CONV_GROUPS
        low = lax.broadcasted_iota(jnp.int32, (tm, LANES), 1) < gw
        sq = ya * ya
        s_lo = jnp.sum(jnp.where(low, sq, 0.0), axis=-1, keepdims=True)
        s_hi = jnp.sum(jnp.where(low, 0.0, sq), axis=-1, keepdims=True)
        ms = jnp.where(low, s_lo, s_hi) * (1.0 / gw)
        ya_ref[:, cols] = (ya * lax.rsqrt(ms + EPS) * gna_ref[:, cols]).astype(ya_ref.dtype)

    pieces = [functools.partial(q_head, hh) for hh in range(MLA_HEADS)] + [kv]
    pieces += [functools.partial(conv_tile, j) for j in range(cwid // LANES)]
    return pieces


def _proj(x1, mod, norm_g, w_main, w_tail, conv_w, gn_a, q_norm_g, wq_t, kv_norm_g, wk, wv_t, cs_row, cs_t,
          tiles_per_batch):
    t, d = x1.shape
    tm = ROW_TILE
    nt = t // tm
    hq = MLA_HEADS * QK_PAD

    def cur(i):
        return jnp.minimum(i, nt - 1)

    def prev(i):
        return jnp.maximum(i - 1, 0)

    return pl.pallas_call(
        functools.partial(_proj_kernel, tiles_per_batch=tiles_per_batch),
        grid=(nt + 1,),
        in_specs=[
            pl.BlockSpec((tm, d), lambda i: (cur(i), 0)),
            pl.BlockSpec((1, N_MOD, d), lambda i: (cur(i) // tiles_per_batch, 0, 0)),
            _resident((1, d)),
            _resident(w_main.shape),
            _resident(w_tail.shape),
            _resident(conv_w.shape),
            _resident(gn_a.shape),
            _resident(q_norm_g.shape),
            _resident(wq_t.shape),
            _resident(kv_norm_g.shape),
            _resident(wk.shape),
            _resident(wv_t.shape),
            pl.BlockSpec((tm, LANES), lambda i: (prev(i), 0)),
            pl.BlockSpec((LANES, tm), lambda i: (0, prev(i))),
        ],
        out_specs=[
            pl.BlockSpec((tm, CONV_WIDTH), lambda i: (prev(i), 0)),
            pl.BlockSpec((1, hq, tm), lambda i: (prev(i), 0, 0)),
            pl.BlockSpec((tm, hq), lambda i: (prev(i), 0)),
            pl.BlockSpec((1, MLA_HEADS * V_ROWS, tm), lambda i: (prev(i), 0, 0)),
        ],
        out_shape=[
            jax.ShapeDtypeStruct((t, CONV_WIDTH), BF16),
            jax.ShapeDtypeStruct((nt, hq, tm), BF16),
            jax.ShapeDtypeStruct((t, hq), BF16),
            jax.ShapeDtypeStruct((nt, MLA_HEADS * V_ROWS, tm), BF16),
        ],
        scratch_shapes=[pltpu.VMEM((tm, w_main.shape[1] + w_tail.shape[1]), F32),
                        pltpu.VMEM((tm, w_main.shape[1] + w_tail.shape[1]), F32),
                        pltpu.VMEM((tm + SUBLANES, CONV_WIDTH), F32)],
        compiler_params=_params(),
        name="mixer_proj",
    )(x1, mod, norm_g, w_main, w_tail, conv_w, gn_a, q_norm_g, wq_t, kv_norm_g, wk, wv_t, cs_row, cs_t)


def _attn_kernel(q_ref, k_ref, v_ref, g_ref, o_ref, s_sc, m_sc, acc_sc):
    i = pl.program_id(1)
    tq = q_ref.shape[2]
    tk = v_ref.shape[2]
    m_sc[...] = jnp.full(m_sc.shape, -jnp.inf, F32)
    acc_sc[...] = jnp.zeros(acc_sc.shape, F32)

    def scores(j, hh, slot):
        row0 = pl.multiple_of(j * tk, tk)
        s_sc[slot] = jnp.dot(k_ref[pl.ds(row0, tk), hh * QK_PAD:(hh + 1) * QK_PAD],
                             q_ref[0, hh * QK_PAD:(hh + 1) * QK_PAD, :],
                             preferred_element_type=F32)

    def softmax_pv(j, hh, slot, masked):
        s = s_sc[slot]
        if masked:
            kc = lax.broadcasted_iota(jnp.int32, (tk, tq), 0) // CHUNK
            qc = lax.broadcasted_iota(jnp.int32, (tk, tq), 1) // CHUNK
            s = jnp.where(kc <= qc, s, -1e30)
        m_prev = m_sc[hh]
        m_new = jnp.maximum(m_prev, jnp.max(s, axis=0, keepdims=True))
        alpha = jnp.exp2(m_prev - m_new)
        p = jnp.exp2(s - m_new).astype(BF16)
        pv = jnp.dot(v_ref[j, hh * V_ROWS:(hh + 1) * V_ROWS, :], p,
                     preferred_element_type=F32)
        acc_sc[hh] = alpha * acc_sc[hh] + pv
        m_sc[hh] = m_new

    scores(0, 0, 0)

    def block(j, masked, next_j):
        for hh in range(MLA_HEADS):
            if hh + 1 < MLA_HEADS:
                scores(j, hh + 1, (hh + 1) % 2)
            elif next_j is not None:
                scores(next_j, 0, 0)
            softmax_pv(j, hh, hh % 2, masked)

    def run(j0, n):
        for t in range(n):
            block(j0 + t, False, j0 + t + 1)

    n_quads = lax.shift_right_logical(i, 2)

    def quad(q, carry):
        run(q * ATTN_UNROLL, ATTN_UNROLL)
        return carry

    lax.fori_loop(0, n_quads, quad, 0)
    done = n_quads * ATTN_UNROLL

    @pl.when((i & 2) != 0)
    def _():
        run(done, 2)

    @pl.when((i & 1) != 0)
    def _():
        run(done + (i & 2), 1)

    block(i, True, None)

    g = g_ref[...]
    for hh in range(MLA_HEADS):
        acc = acc_sc[hh]
        o_t = acc[0:V_HEAD] / acc[V_HEAD:V_HEAD + 1]
        o_t = o_t * lax.rsqrt(jnp.mean(o_t * o_t, axis=0, keepdims=True) + EPS)
        o_ref[:, hh * V_HEAD:(hh + 1) * V_HEAD] = (
            o_t.T * g[:, hh * V_HEAD:(hh + 1) * V_HEAD]).astype(o_ref.dtype)


def _attention(q_t, k, v_t, gn_b, bsz, seq):
    tq = ATTN_TQ
    tk = v_t.shape[2]
    assert tq == tk
    nq = seq // tq
    hq = MLA_HEADS * QK_PAD
    return pl.pallas_call(
        _attn_kernel,
        grid=(bsz, nq),
        in_specs=[
            pl.BlockSpec((1, hq, tq), lambda b, i: (b * nq + i, 0, 0)),
            pl.BlockSpec((seq, hq), lambda b, i: (b, 0), pipeline_mode=pl.Buffered(1)),
            pl.BlockSpec((seq // tk, MLA_HEADS * V_ROWS, tk), lambda b, i: (b, 0, 0),
                         pipeline_mode=pl.Buffered(1)),
            _resident(gn_b.shape),
        ],
        out_specs=pl.BlockSpec((tq, MLA_WIDTH), lambda b, i: (b * nq + i, 0)),
        out_shape=jax.ShapeDtypeStruct((bsz * seq, MLA_WIDTH), BF16),
        scratch_shapes=[
            pltpu.VMEM((2, tk, tq), F32),
            pltpu.VMEM((MLA_HEADS, 1, tq), F32),
            pltpu.VMEM((MLA_HEADS, V_ROWS, tq), F32),
        ],
        compiler_params=_params(2),
        name="mla_attention",
    )(q_t, k, v_t, gn_b)


def _out_kernel(x_ref, ya_ref, yb_ref, mod_ref, wo_ref, ng_ref, w1_ref, w3_ref, w2_ref, fg_ref, o_ref):
    mod = mod_ref[0]
    y = jnp.concatenate([ya_ref[...], yb_ref[...]], axis=1)
    x = x_ref[...] + mod[5:6] * jnp.dot(y, wo_ref[...], preferred_element_type=F32)
    x = _ffn_update(x, mod[6:7], mod[7:8], mod[8:9], ng_ref[...], w1_ref, w3_ref, w2_ref)
    o_ref[...] = x * _rms_scale(x) * fg_ref[...]


def _out_ffn2(x1, ya, yb, mod, w_out, norm_g, w1, w3, w2, final_g, tiles_per_batch):
    t, d = x1.shape
    tm = ROW_TILE
    return pl.pallas_call(
        _out_kernel,
        grid=(t // tm,),
        in_specs=[
            pl.BlockSpec((tm, d), lambda i: (i, 0)),
            pl.BlockSpec((tm, CONV_WIDTH), lambda i: (i, 0)),
            pl.BlockSpec((tm, MLA_WIDTH), lambda i: (i, 0)),
            pl.BlockSpec((1, N_MOD, d), lambda i: (i // tiles_per_batch, 0, 0)),
            _resident(w_out.shape),
            _resident((1, d)),
            _resident(w1.shape), _resident(w3.shape), _resident(w2.shape),
            _resident((1, d)),
        ],
        out_specs=pl.BlockSpec((tm, d), lambda i: (i, 0)),
        out_shape=jax.ShapeDtypeStruct((t, d), F32),
        compiler_params=_params(),
        name="out_ffn2",
    )(x1, ya, yb, mod, w_out, norm_g, w1, w3, w2, final_g)


def _swap_halves(w, axis):
    lo, hi = jnp.split(w, 2, axis=axis)
    return jnp.concatenate([hi, lo], axis=axis)


def kernel(x, c, positions, ada_w, ada_b, norm_ffn1_g, ffn1_w1, ffn1_w3, ffn1_w2, norm_mix_g, w_in, conv_w, q_norm_g, w_uq, kv_norm_g, w_ukv, out_norm_g, w_out, norm_ffn2_g, ffn2_w1, ffn2_w3, ffn2_w2, final_norm_g):
    bsz, seq, d = x.shape
    depth = ada_w.shape[0]
    t = bsz * seq
    tiles_per_batch = seq // ROW_TILE
    half = QK_ROPE // 2

    assert depth == 1, "only the one-layer block is implemented"
    l = 0

    cs_t, cs_row, mod, (f1w1, f1w3, f1w2) = _prologue(
        positions, c, ada_w[l], ada_b[l], [ffn1_w1[l], ffn1_w3[l], ffn1_w2[l]])
    mod = mod.reshape(bsz, N_MOD, d)

    xf = x.reshape(t, d)

    wq = w_uq[l].T.reshape(MLA_HEADS, QK_NOPE + QK_ROPE, Q_LORA)
    wq_rope = wq[:, QK_NOPE:]
    wq_t = jnp.concatenate([wq, _swap_halves(wq_rope, 1)], axis=1).reshape(
        MLA_HEADS * QK_PAD, Q_LORA).astype(BF16)
    wkv = w_ukv[l].reshape(KV_LORA, MLA_HEADS, QK_NOPE + V_HEAD)
    wk = wkv[:, :, :QK_NOPE].reshape(KV_LORA, MLA_HEADS * QK_NOPE).astype(BF16)
    wv_t = wkv[:, :, QK_NOPE:].reshape(KV_LORA, MLA_WIDTH).T.astype(BF16)

    x1, w_main, w_tail, (wo, f2w1, f2w3, f2w2) = _ffn1(
        xf, mod, norm_ffn1_g[l].reshape(1, d), f1w1, f1w3, f1w2, tiles_per_batch,
        w_in[l], [w_out[l], ffn2_w1[l], ffn2_w3[l], ffn2_w2[l]])
    ya, q_t, k, v_t = _proj(
        x1, mod, norm_mix_g[l].reshape(1, d), w_main, w_tail, conv_w[l],
        out_norm_g[l, :CONV_WIDTH].reshape(1, CONV_WIDTH), q_norm_g[l].reshape(1, Q_LORA), wq_t,
        kv_norm_g[l].reshape(1, KV_LORA), wk, wv_t, cs_row, cs_t, tiles_per_batch)
    yb = _attention(q_t, k, v_t, out_norm_g[l, CONV_WIDTH:].reshape(1, MLA_WIDTH), bsz, seq)
    xf = _out_ffn2(x1, ya, yb, mod, wo, norm_ffn2_g[l].reshape(1, d), f2w1, f2w3, f2w2,
                   final_norm_g.reshape(1, d), tiles_per_batch)
    return xf.reshape(bsz, seq, d)
```

```python
import functools
import math

import jax
import jax.numpy as jnp
from jax import lax
from jax.experimental import pallas as pl
from jax.experimental.pallas import tpu as pltpu

CHUNK = 64
EPS = 1e-6
N_MOD = 9
CONV_WIDTH = 512
CONV_GROUPS = 8
CONV_K = 3
MLA_HEADS = 4
QK_NOPE = 128
QK_ROPE = 64
V_HEAD = 128
Q_LORA = 384
KV_LORA = 256
ROPE_THETA = 10000.0
MLA_WIDTH = MLA_HEADS * V_HEAD

LANES = 128
SUBLANES = 8
MXU_TILE = 256
QK_PAD = MXU_TILE
BF16_ROWS = 16
V_ROWS = V_HEAD + BF16_ROWS
VMEM_LIMIT_BYTES = 56 * 1024 * 1024

ROW_TILE = 512
FFN_STEP_ROWS = 1024
ATTN_TQ = 512
ATTN_UNROLL = 4
PROLOGUE_STEPS = 8

F32 = jnp.float32
BF16 = jnp.bfloat16


def _params(n_axes=1, flags=None):
    return pltpu.CompilerParams(
        dimension_semantics=("arbitrary",) * n_axes,
        vmem_limit_bytes=VMEM_LIMIT_BYTES,
        flags=flags,
    )


def _resident(shape):
    nd = len(shape)
    return pl.BlockSpec(shape, lambda *_: (0,) * nd, pipeline_mode=pl.Buffered(1))


def _rms_scale(x):
    return lax.rsqrt(jnp.mean(x * x, axis=-1, keepdims=True) + EPS)


def _cast_streams(weights, n_steps):
    in_specs, out_specs, out_shapes = [], [], []
    for w in weights:
        rows, width = w.shape
        s = max(k for k in range(1, n_steps + 1)
                if rows % k == 0 and (rows // k) % BF16_ROWS == 0)
        spec = pl.BlockSpec((rows // s, width), lambda i, s=s: (jnp.minimum(i, s - 1), 0))
        in_specs.append(spec)
        out_specs.append(spec)
        out_shapes.append(jax.ShapeDtypeStruct((rows, width), BF16))
    return in_specs, out_specs, out_shapes


def _run_casts(src_refs, dst_refs):
    for src, dst in zip(src_refs, dst_refs):
        dst[...] = src[...].astype(dst.dtype)


def _prologue_kernel(pos_ref, inv_ref, cb_ref, w_ref, b_ref, *refs):
    n_cast = (len(refs) - 3) // 2
    cst_ref, csr_ref, mod_ref = refs[n_cast:n_cast + 3]

    ang = pos_ref[...].astype(F32) * inv_ref[...]
    c = jnp.cos(ang)
    s = jnp.sin(ang)
    cst = jnp.concatenate([c, c, -s, s], axis=0)
    cst_ref[...] = cst
    csr_ref[...] = cst.T

    w = w_ref[...]
    tn = w.shape[1]
    for b in range(cb_ref.shape[0]):
        cb = cb_ref[b]
        act = cb * jax.nn.sigmoid(cb)
        cols = [jnp.sum(w[:, j * LANES:(j + 1) * LANES] * act, axis=0, keepdims=True)
                for j in range(tn // LANES)]
        mod_ref[b:b + 1, :] = jnp.concatenate(cols, axis=1) + b_ref[...]

    _run_casts(refs[:n_cast], refs[n_cast + 3:])


def _prologue(positions, c, ada_w, ada_b, cast_weights):
    half = QK_ROPE // 2
    t = positions.size
    n_steps = PROLOGUE_STEPS
    tc = t // n_steps
    bsz, d = c.shape
    n = ada_w.shape[1]
    tn = n // n_steps
    assert t % n_steps == 0 and n % n_steps == 0 and tn % LANES == 0 and tc % LANES == 0
    inv_freq = ROPE_THETA ** (-jnp.arange(0, QK_ROPE, 2, dtype=F32) / QK_ROPE)
    cb = jnp.broadcast_to(c[:, :, None], (bsz, d, LANES))
    c_in, c_out, c_shapes = _cast_streams(cast_weights, n_steps)
    outs = pl.pallas_call(
        _prologue_kernel,
        grid=(n_steps,),
        in_specs=[pl.BlockSpec((1, tc), lambda i: (0, i)),
                  pl.BlockSpec((half, 1), lambda i: (0, 0)),
                  pl.BlockSpec((bsz, d, LANES), lambda i: (0, 0, 0)),
                  pl.BlockSpec((d, tn), lambda i: (0, i)),
                  pl.BlockSpec((1, tn), lambda i: (0, i))] + c_in,
        out_specs=[pl.BlockSpec((4 * half, tc), lambda i: (0, i)),
                   pl.BlockSpec((tc, 4 * half), lambda i: (i, 0)),
                   pl.BlockSpec((bsz, tn), lambda i: (0, i))] + c_out,
        out_shape=[jax.ShapeDtypeStruct((4 * half, t), F32),
                   jax.ShapeDtypeStruct((t, 4 * half), F32),
                   jax.ShapeDtypeStruct((bsz, n), F32)] + c_shapes,
        compiler_params=_params(),
        name="prologue",
    )(positions.reshape(1, t), inv_freq.reshape(half, 1), cb, ada_w, ada_b.reshape(1, n), *cast_weights)
    return outs[0], outs[1], outs[2], outs[3:]


def _ffn_update(x, shift, scale, gate, norm_g, w1_ref, w3_ref, w2_ref):
    h = (x * _rms_scale(x) * (norm_g * (1.0 + scale)) + shift).astype(BF16)
    a = jnp.dot(h, w1_ref[...], preferred_element_type=F32)
    b = jnp.dot(h, w3_ref[...], preferred_element_type=F32)
    g = (a * jax.nn.sigmoid(a) * b).astype(BF16)
    o = jnp.dot(g, w2_ref[...], preferred_element_type=F32)
    return x + (0.5 * gate) * o


def _ffn1_kernel(x_ref, mod_ref, ng_ref, w1_ref, w3_ref, w2_ref, win_ref, *refs):
    n_cast = (len(refs) - 3) // 2
    o_ref, wmain_ref, wtail_ref = refs[n_cast:n_cast + 3]
    mod = mod_ref[0]
    for r in range(0, x_ref.shape[0], ROW_TILE):
        rows = slice(r, r + ROW_TILE)
        o_ref[rows, :] = _ffn_update(x_ref[rows, :], mod[0:1], mod[1:2], mod[2:3], ng_ref[...],
                                     w1_ref, w3_ref, w2_ref)
    _run_casts(refs[:n_cast], refs[n_cast + 3:])
    w = win_ref[...]
    n_main = wmain_ref.shape[1]
    kr0 = w.shape[1] - QK_ROPE
    half = QK_ROPE // 2
    wmain_ref[...] = w[:, :n_main].astype(BF16)
    wtail_ref[...] = jnp.concatenate(
        [w[:, n_main:], w[:, kr0 + half:], w[:, kr0:kr0 + half]], axis=1).astype(BF16)


def _ffn1(x2d, mod, norm_g, w1, w3, w2, tiles_per_batch, w_in, cast_weights):
    t, d = x2d.shape
    tm = FFN_STEP_ROWS
    tiles_per_batch = tiles_per_batch * ROW_TILE // tm
    n_steps = t // tm
    c_in, c_out, c_shapes = _cast_streams(cast_weights, n_steps)
    rows, in_cols = w_in.shape
    n_main = (in_cols // MXU_TILE) * MXU_TILE
    assert in_cols - n_main + QK_ROPE == MXU_TILE and rows % n_steps == 0
    wr = rows // n_steps
    outs = pl.pallas_call(
        _ffn1_kernel,
        grid=(n_steps,),
        in_specs=[
            pl.BlockSpec((tm, d), lambda i: (i, 0)),
            pl.BlockSpec((1, N_MOD, d), lambda i: (i // tiles_per_batch, 0, 0)),
            _resident((1, d)),
            _resident(w1.shape), _resident(w3.shape), _resident(w2.shape),
            pl.BlockSpec((wr, in_cols), lambda i: (i, 0)),
        ] + c_in,
        out_specs=[pl.BlockSpec((tm, d), lambda i: (i, 0)),
                   pl.BlockSpec((wr, n_main), lambda i: (i, 0)),
                   pl.BlockSpec((wr, MXU_TILE), lambda i: (i, 0))] + c_out,
        out_shape=[jax.ShapeDtypeStruct((t, d), F32),
                   jax.ShapeDtypeStruct((rows, n_main), BF16),
                   jax.ShapeDtypeStruct((rows, MXU_TILE), BF16)] + c_shapes,
        compiler_params=_params(),
        name="ffn1",
    )(x2d, mod, norm_g, w1, w3, w2, w_in, *cast_weights)
    return outs[0], outs[1], outs[2], outs[3:]


def _proj_kernel(x_ref, mod_ref, ng_ref, win_ref, wtail_ref, cw_ref, gna_ref, qng_ref, wq_ref, kvng_ref,
                 wk_ref, wv_ref, cs_ref, cst_ref,
                 ya_ref, q_ref, k_ref, v_ref, zbuf_a, zbuf_b, ubuf, *, tiles_per_batch):
    i = pl.program_id(0)

    @pl.when(i == 0)
    def _():
        zbuf_b[...] = jnp.zeros(zbuf_b.shape, F32)
        ubuf[0:SUBLANES, :] = jnp.zeros((SUBLANES, CONV_WIDTH), F32)

    def step(z_new, z_old):
        x = x_ref[...]
        mod = mod_ref[0]
        h = (x * _rms_scale(x) * (ng_ref[...] * (1.0 + mod[4:5])) + mod[3:4]).astype(BF16)
        post = _mixer_post_pieces(z_old, i - 1, cw_ref, gna_ref, qng_ref, wq_ref, kvng_ref, wk_ref,
                                  wv_ref, cs_ref, cst_ref, ya_ref, q_ref, k_ref, v_ref, ubuf,
                                  tiles_per_batch)
        n_main = win_ref.shape[1] // MXU_TILE
        assert wtail_ref.shape[1] == MXU_TILE and len(post) == n_main + 1
        for c in range(n_main + 1):
            cols = slice(c * MXU_TILE, (c + 1) * MXU_TILE)
            w_c = win_ref[:, cols] if c < n_main else wtail_ref[...]
            z_new[:, cols] = jnp.dot(h, w_c, preferred_element_type=F32)
            post[c]()

    pl.when(i % 2 == 0)(functools.partial(step, zbuf_a, zbuf_b))
    pl.when(i % 2 == 1)(functools.partial(step, zbuf_b, zbuf_a))


def _mixer_post_pieces(z, tile, cw_ref, gna_ref, qng_ref, wq_ref, kvng_ref, wk_ref, wv_ref, cs_ref,
                       cst_ref, ya_ref, q_ref, k_ref, v_ref, ubuf, tiles_per_batch):
    tm = z.shape[0]
    cwid = CONV_WIDTH
    c0 = 3 * cwid
    nt = (((1,), (1,)), ((), ()))
    scale = (QK_NOPE + QK_ROPE) ** -0.5 * math.log2(math.e)
    state = {}

    def q_head(hh):
        if hh == 0:
            cq = z[:, c0:c0 + Q_LORA]
            state["cqn"] = (cq * _rms_scale(cq) * qng_ref[...]).astype(BF16)
        base = hh * QK_PAD
        qt = lax.dot_general(wq_ref[base:base + QK_PAD, :], state["cqn"], nt,
                             preferred_element_type=F32)
        ct = cst_ref[0:QK_ROPE, :]
        st = cst_ref[QK_ROPE:2 * QK_ROPE, :]
        rope = qt[QK_NOPE:QK_NOPE + QK_ROPE] * ct + qt[QK_NOPE + QK_ROPE:QK_PAD] * st
        q_ref[0, base:base + QK_PAD, :] = jnp.concatenate(
            [qt[0:QK_NOPE] * scale, rope * scale, jnp.zeros((QK_ROPE, tm), F32)],
            axis=0).astype(q_ref.dtype)

    def kv():
        ckv = z[:, c0 + Q_LORA:c0 + Q_LORA + KV_LORA]
        ckvn = (ckv * _rms_scale(ckv) * kvng_ref[...]).astype(BF16)
        kn = jnp.dot(ckvn, wk_ref[...], preferred_element_type=F32)
        krr = z[:, c0 + Q_LORA + KV_LORA:c0 + Q_LORA + KV_LORA + LANES]
        a = krr * cs_ref[...]
        kr = a + pltpu.roll(a, QK_ROPE, axis=1)
        kp = []
        for hh in range(MLA_HEADS):
            kp += [kn[:, hh * QK_NOPE:(hh + 1) * QK_NOPE], kr]
        k_ref[...] = jnp.concatenate(kp, axis=1).astype(k_ref.dtype)
        vt = lax.dot_general(wv_ref[...], ckvn, nt, preferred_element_type=F32)
        ones = jnp.ones((V_ROWS - V_HEAD, tm), F32)
        vp = []
        for hh in range(MLA_HEADS):
            vp += [vt[hh * V_HEAD:(hh + 1) * V_HEAD], ones]
        v_ref[0] = jnp.concatenate(vp, axis=0).astype(v_ref.dtype)

    def conv_tile(j):
        cols = slice(j * LANES, (j + 1) * LANES)
        ubuf[0:SUBLANES, cols] = jnp.where(tile % tiles_per_batch == 0, 0.0, ubuf[0:SUBLANES, cols])
        u = z[:, cwid + j * LANES:cwid + (j + 1) * LANES] * z[:, 2 * cwid + j * LANES:2 * cwid + (j + 1) * LANES]
        ubuf[SUBLANES:SUBLANES + tm, cols] = u
        u1 = ubuf[SUBLANES - 1:SUBLANES - 1 + tm, cols]
        u2 = ubuf[SUBLANES - 2:SUBLANES - 2 + tm, cols]
        cw = cw_ref[:, cols]
        ya = z[:, cols] * (cw[0:1] * u2 + cw[1:2] * u1 + cw[2:3] * u)
        ubuf[0:SUBLANES, cols] = ubuf[tm:tm + SUBLANES, cols]
        gw = cwid // CONV_GROUPS
        low = lax.broadcasted_iota(jnp.int32, (tm, LANES), 1) < gw
        sq = ya * ya
        s_lo = jnp.sum(jnp.where(low, sq, 0.0), axis=-1, keepdims=True)
        s_hi = jnp.sum(jnp.where(low, 0.0, sq), axis=-1, keepdims=True)
        ms = jnp.where(low, s_lo, s_hi) * (1.0 / gw)
        ya_ref[:, cols] = (ya * lax.rsqrt(ms + EPS) * gna_ref[:, cols]).astype(ya_ref.dtype)

    pieces = [functools.partial(q_head, hh) for hh in range(MLA_HEADS)] + [kv]
    pieces += [functools.partial(conv_tile, j) for j in range(cwid // LANES)]
    return pieces


def _proj(x1, mod, norm_g, w_main, w_tail, conv_w, gn_a, q_norm_g, wq_t, kv_norm_g, wk, wv_t, cs_row, cs_t,
          tiles_per_batch):
    t, d = x1.shape
    tm = ROW_TILE
    nt = t // tm
    hq = MLA_HEADS * QK_PAD

    def cur(i):
        return jnp.minimum(i, nt - 1)

    def prev(i):
        return jnp.maximum(i - 1, 0)

    return pl.pallas_call(
        functools.partial(_proj_kernel, tiles_per_batch=tiles_per_batch),
        grid=(nt + 1,),
        in_specs=[
            pl.BlockSpec((tm, d), lambda i: (cur(i), 0)),
            pl.BlockSpec((1, N_MOD, d), lambda i: (cur(i) // tiles_per_batch, 0, 0)),
            _resident((1, d)),
            _resident(w_main.shape),
            _resident(w_tail.shape),
            _resident(conv_w.shape),
            _resident(gn_a.shape),
            _resident(q_norm_g.shape),
            _resident(wq_t.shape),
            _resident(kv_norm_g.shape),
            _resident(wk.shape),
            _resident(wv_t.shape),
            pl.BlockSpec((tm, LANES), lambda i: (prev(i), 0)),
            pl.BlockSpec((LANES, tm), lambda i: (0, prev(i))),
        ],
        out_specs=[
            pl.BlockSpec((tm, CONV_WIDTH), lambda i: (prev(i), 0)),
            pl.BlockSpec((1, hq, tm), lambda i: (prev(i), 0, 0)),
            pl.BlockSpec((tm, hq), lambda i: (prev(i), 0)),
            pl.BlockSpec((1, MLA_HEADS * V_ROWS, tm), lambda i: (prev(i), 0, 0)),
        ],
        out_shape=[
            jax.ShapeDtypeStruct((t, CONV_WIDTH), BF16),
            jax.ShapeDtypeStruct((nt, hq, tm), BF16),
            jax.ShapeDtypeStruct((t, hq), BF16),
            jax.ShapeDtypeStruct((nt, MLA_HEADS * V_ROWS, tm), BF16),
        ],
        scratch_shapes=[pltpu.VMEM((tm, w_main.shape[1] + w_tail.shape[1]), F32),
                        pltpu.VMEM((tm, w_main.shape[1] + w_tail.shape[1]), F32),
                        pltpu.VMEM((tm + SUBLANES, CONV_WIDTH), F32)],
        compiler_params=_params(),
        name="mixer_proj",
    )(x1, mod, norm_g, w_main, w_tail, conv_w, gn_a, q_norm_g, wq_t, kv_norm_g, wk, wv_t, cs_row, cs_t)


def _attn_kernel(q_ref, k_ref, v_ref, g_ref, o_ref, s_sc, m_sc, acc_sc):
    i = pl.program_id(1)
    tq = q_ref.shape[2]
    tk = v_ref.shape[2]
    m_sc[...] = jnp.full(m_sc.shape, -jnp.inf, F32)
    acc_sc[...] = jnp.zeros(acc_sc.shape, F32)

    def scores(j, hh, slot):
        row0 = pl.multiple_of(j * tk, tk)
        s_sc[slot] = jnp.dot(k_ref[pl.ds(row0, tk), hh * QK_PAD:(hh + 1) * QK_PAD],
                             q_ref[0, hh * QK_PAD:(hh + 1) * QK_PAD, :],
                             preferred_element_type=F32)

    def softmax_pv(j, hh, slot, masked):
        s = s_sc[slot]
        if masked:
            kc = lax.broadcasted_iota(jnp.int32, (tk, tq), 0) // CHUNK
            qc = lax.broadcasted_iota(jnp.int32, (tk, tq), 1) // CHUNK
            s = jnp.where(kc <= qc, s, -1e30)
        m_prev = m_sc[hh]
        m_new = jnp.maximum(m_prev, jnp.max(s, axis=0, keepdims=True))
        alpha = jnp.exp2(m_prev - m_new)
        p = jnp.exp2(s - m_new).astype(BF16)
        pv = jnp.dot(v_ref[j, hh * V_ROWS:(hh + 1) * V_ROWS, :], p,
                     preferred_element_type=F32)
        acc_sc[hh] = alpha * acc_sc[hh] + pv
        m_sc[hh] = m_new

    scores(0, 0, 0)

    def block(j, masked, next_j):
        for hh in range(MLA_HEADS):
            if hh + 1 < MLA_HEADS:
                scores(j, hh + 1, (hh + 1) % 2)
            elif next_j is not None:
                scores(next_j, 0, 0)
            softmax_pv(j, hh, hh % 2, masked)

    def run(j0, n):
        for t in range(n):
            block(j0 + t, False, j0 + t + 1)

    n_quads = lax.shift_right_logical(i, 2)

    def quad(q, carry):
        run(q * ATTN_UNROLL, ATTN_UNROLL)
        return carry

    lax.fori_loop(0, n_quads, quad, 0)
    done = n_quads * ATTN_UNROLL

    @pl.when((i & 2) != 0)
    def _():
        run(done, 2)

    @pl.when((i & 1) != 0)
    def _():
        run(done + (i & 2), 1)

    block(i, True, None)

    g = g_ref[...]
    for hh in range(MLA_HEADS):
        acc = acc_sc[hh]
        o_t = acc[0:V_HEAD] / acc[V_HEAD:V_HEAD + 1]
        o_t = o_t * lax.rsqrt(jnp.mean(o_t * o_t, axis=0, keepdims=True) + EPS)
        o_ref[:, hh * V_HEAD:(hh + 1) * V_HEAD] = (
            o_t.T * g[:, hh * V_HEAD:(hh + 1) * V_HEAD]).astype(o_ref.dtype)


def _attention(q_t, k, v_t, gn_b, bsz, seq):
    tq = ATTN_TQ
    tk = v_t.shape[2]
    assert tq == tk
    nq = seq // tq
    hq = MLA_HEADS * QK_PAD
    return pl.pallas_call(
        _attn_kernel,
        grid=(bsz, nq),
        in_specs=[
            pl.BlockSpec((1, hq, tq), lambda b, i: (b * nq + i, 0, 0)),
            pl.BlockSpec((seq, hq), lambda b, i: (b, 0), pipeline_mode=pl.Buffered(1)),
            pl.BlockSpec((seq // tk, MLA_HEADS * V_ROWS, tk), lambda b, i: (b, 0, 0),
                         pipeline_mode=pl.Buffered(1)),
            _resident(gn_b.shape),
        ],
        out_specs=pl.BlockSpec((tq, MLA_WIDTH), lambda b, i: (b * nq + i, 0)),
        out_shape=jax.ShapeDtypeStruct((bsz * seq, MLA_WIDTH), BF16),
        scratch_shapes=[
            pltpu.VMEM((2, tk, tq), F32),
            pltpu.VMEM((MLA_HEADS, 1, tq), F32),
            pltpu.VMEM((MLA_HEADS, V_ROWS, tq), F32),
        ],
        compiler_params=_params(2),
        name="mla_attention",
    )(q_t, k, v_t, gn_b)


def _out_kernel(x_ref, ya_ref, yb_ref, mod_ref, wo_ref, ng_ref, w1_ref, w3_ref, w2_ref, fg_ref, o_ref):
    mod = mod_ref[0]
    for r in range(0, x_ref.shape[0], ROW_TILE):
        rows = slice(r, r + ROW_TILE)
        y = jnp.concatenate([ya_ref[rows, :], yb_ref[rows, :]], axis=1)
        x = x_ref[rows, :] + mod[5:6] * jnp.dot(y, wo_ref[...], preferred_element_type=F32)
        x = _ffn_update(x, mod[6:7], mod[7:8], mod[8:9], ng_ref[...], w1_ref, w3_ref, w2_ref)
        o_ref[rows, :] = x * _rms_scale(x) * fg_ref[...]


def _out_ffn2(x1, ya, yb, mod, w_out, norm_g, w1, w3, w2, final_g, tiles_per_batch):
    t, d = x1.shape
    tm = FFN_STEP_ROWS
    tiles_per_batch = tiles_per_batch * ROW_TILE // tm
    return pl.pallas_call(
        _out_kernel,
        grid=(t // tm,),
        in_specs=[
            pl.BlockSpec((tm, d), lambda i: (i, 0)),
            pl.BlockSpec((tm, CONV_WIDTH), lambda i: (i, 0)),
            pl.BlockSpec((tm, MLA_WIDTH), lambda i: (i, 0)),
            pl.BlockSpec((1, N_MOD, d), lambda i: (i // tiles_per_batch, 0, 0)),
            _resident(w_out.shape),
            _resident((1, d)),
            _resident(w1.shape), _resident(w3.shape), _resident(w2.shape),
            _resident((1, d)),
        ],
        out_specs=pl.BlockSpec((tm, d), lambda i: (i, 0)),
        out_shape=jax.ShapeDtypeStruct((t, d), F32),
        compiler_params=_params(),
        name="out_ffn2",
    )(x1, ya, yb, mod, w_out, norm_g, w1, w3, w2, final_g)


def _swap_halves(w, axis):
    lo, hi = jnp.split(w, 2, axis=axis)
    return jnp.concatenate([hi, lo], axis=axis)


def kernel(x, c, positions, ada_w, ada_b, norm_ffn1_g, ffn1_w1, ffn1_w3, ffn1_w2, norm_mix_g, w_in, conv_w, q_norm_g, w_uq, kv_norm_g, w_ukv, out_norm_g, w_out, norm_ffn2_g, ffn2_w1, ffn2_w3, ffn2_w2, final_norm_g):
    bsz, seq, d = x.shape
    depth = ada_w.shape[0]
    t = bsz * seq
    tiles_per_batch = seq // ROW_TILE
    half = QK_ROPE // 2

    assert depth == 1, "only the one-layer block is implemented"
    l = 0

    cs_t, cs_row, mod, (f1w1, f1w3, f1w2) = _prologue(
        positions, c, ada_w[l], ada_b[l], [ffn1_w1[l], ffn1_w3[l], ffn1_w2[l]])
    mod = mod.reshape(bsz, N_MOD, d)

    xf = x.reshape(t, d)

    wq = w_uq[l].T.reshape(MLA_HEADS, QK_NOPE + QK_ROPE, Q_LORA)
    wq_rope = wq[:, QK_NOPE:]
    wq_t = jnp.concatenate([wq, _swap_halves(wq_rope, 1)], axis=1).reshape(
        MLA_HEADS * QK_PAD, Q_LORA).astype(BF16)
    wkv = w_ukv[l].reshape(KV_LORA, MLA_HEADS, QK_NOPE + V_HEAD)
    wk = wkv[:, :, :QK_NOPE].reshape(KV_LORA, MLA_HEADS * QK_NOPE).astype(BF16)
    wv_t = wkv[:, :, QK_NOPE:].reshape(KV_LORA, MLA_WIDTH).T.astype(BF16)

    x1, w_main, w_tail, (wo, f2w1, f2w3, f2w2) = _ffn1(
        xf, mod, norm_ffn1_g[l].reshape(1, d), f1w1, f1w3, f1w2, tiles_per_batch,
        w_in[l], [w_out[l], ffn2_w1[l], ffn2_w3[l], ffn2_w2[l]])
    ya, q_t, k, v_t = _proj(
        x1, mod, norm_mix_g[l].reshape(1, d), w_main, w_tail, conv_w[l],
        out_norm_g[l, :CONV_WIDTH].reshape(1, CONV_WIDTH), q_norm_g[l].reshape(1, Q_LORA), wq_t,
        kv_norm_g[l].reshape(1, KV_LORA), wk, wv_t, cs_row, cs_t, tiles_per_batch)
    yb = _attention(q_t, k, v_t, out_norm_g[l, CONV_WIDTH:].reshape(1, MLA_WIDTH), bsz, seq)
    xf = _out_ffn2(x1, ya, yb, mod, wo, norm_ffn2_g[l].reshape(1, d), f2w1, f2w3, f2w2,
                   final_norm_g.reshape(1, d), tiles_per_batch)
    return xf.reshape(bsz, seq, d)
```

```python
import functools
import math

import jax
import jax.numpy as jnp
from jax import lax
from jax.experimental import pallas as pl
from jax.experimental.pallas import tpu as pltpu

CHUNK = 64
EPS = 1e-6
N_MOD = 9
CONV_WIDTH = 512
CONV_GROUPS = 8
CONV_K = 3
MLA_HEADS = 4
QK_NOPE = 128
QK_ROPE = 64
V_HEAD = 128
Q_LORA = 384
KV_LORA = 256
ROPE_THETA = 10000.0
MLA_WIDTH = MLA_HEADS * V_HEAD

LANES = 128
SUBLANES = 8
MXU_TILE = 256
QK_PAD = MXU_TILE
BF16_ROWS = 16
V_ROWS = V_HEAD + BF16_ROWS
VMEM_LIMIT_BYTES = 56 * 1024 * 1024

ROW_TILE = 512
FFN_STEP_ROWS = 1024
ATTN_TQ = 512
ATTN_UNROLL = 4
PROLOGUE_STEPS = 8

F32 = jnp.float32
BF16 = jnp.bfloat16


def _params(n_axes=1, flags=None):
    return pltpu.CompilerParams(
        dimension_semantics=("arbitrary",) * n_axes,
        vmem_limit_bytes=VMEM_LIMIT_BYTES,
        flags=flags,
    )


def _resident(shape):
    nd = len(shape)
    return pl.BlockSpec(shape, lambda *_: (0,) * nd, pipeline_mode=pl.Buffered(1))


def _rms_scale(x):
    return lax.rsqrt(jnp.mean(x * x, axis=-1, keepdims=True) + EPS)


def _cast_streams(weights, n_steps):
    in_specs, out_specs, out_shapes = [], [], []
    for w in weights:
        rows, width = w.shape
        s = max(k for k in range(1, n_steps + 1)
                if rows % k == 0 and (rows // k) % BF16_ROWS == 0)
        spec = pl.BlockSpec((rows // s, width), lambda i, s=s: (jnp.minimum(i, s - 1), 0))
        in_specs.append(spec)
        out_specs.append(spec)
        out_shapes.append(jax.ShapeDtypeStruct((rows, width), BF16))
    return in_specs, out_specs, out_shapes


def _run_casts(src_refs, dst_refs):
    for src, dst in zip(src_refs, dst_refs):
        dst[...] = src[...].astype(dst.dtype)


def _prologue_kernel(pos_ref, inv_ref, cb_ref, w_ref, b_ref, *refs):
    n_cast = (len(refs) - 3) // 2
    cst_ref, csr_ref, mod_ref = refs[n_cast:n_cast + 3]

    ang = pos_ref[...].astype(F32) * inv_ref[...]
    c = jnp.cos(ang)
    s = jnp.sin(ang)
    cst = jnp.concatenate([c, c, -s, s], axis=0)
    cst_ref[...] = cst
    csr_ref[...] = cst.T

    w = w_ref[...]
    tn = w.shape[1]
    for b in range(cb_ref.shape[0]):
        cb = cb_ref[b]
        act = cb * jax.nn.sigmoid(cb)
        cols = [jnp.sum(w[:, j * LANES:(j + 1) * LANES] * act, axis=0, keepdims=True)
                for j in range(tn // LANES)]
        mod_ref[b:b + 1, :] = jnp.concatenate(cols, axis=1) + b_ref[...]

    _run_casts(refs[:n_cast], refs[n_cast + 3:])


def _prologue(positions, c, ada_w, ada_b, cast_weights):
    half = QK_ROPE // 2
    t = positions.size
    n_steps = PROLOGUE_STEPS
    tc = t // n_steps
    bsz, d = c.shape
    n = ada_w.shape[1]
    tn = n // n_steps
    assert t % n_steps == 0 and n % n_steps == 0 and tn % LANES == 0 and tc % LANES == 0
    inv_freq = ROPE_THETA ** (-jnp.arange(0, QK_ROPE, 2, dtype=F32) / QK_ROPE)
    cb = jnp.broadcast_to(c[:, :, None], (bsz, d, LANES))
    c_in, c_out, c_shapes = _cast_streams(cast_weights, n_steps)
    outs = pl.pallas_call(
        _prologue_kernel,
        grid=(n_steps,),
        in_specs=[pl.BlockSpec((1, tc), lambda i: (0, i)),
                  pl.BlockSpec((half, 1), lambda i: (0, 0)),
                  pl.BlockSpec((bsz, d, LANES), lambda i: (0, 0, 0)),
                  pl.BlockSpec((d, tn), lambda i: (0, i)),
                  pl.BlockSpec((1, tn), lambda i: (0, i))] + c_in,
        out_specs=[pl.BlockSpec((4 * half, tc), lambda i: (0, i)),
                   pl.BlockSpec((tc, 4 * half), lambda i: (i, 0)),
                   pl.BlockSpec((bsz, tn), lambda i: (0, i))] + c_out,
        out_shape=[jax.ShapeDtypeStruct((4 * half, t), F32),
                   jax.ShapeDtypeStruct((t, 4 * half), F32),
                   jax.ShapeDtypeStruct((bsz, n), F32)] + c_shapes,
        compiler_params=_params(),
        name="prologue",
    )(positions.reshape(1, t), inv_freq.reshape(half, 1), cb, ada_w, ada_b.reshape(1, n), *cast_weights)
    return outs[0], outs[1], outs[2], outs[3:]


def _ffn_update(x, shift, scale, gate, norm_g, w1_ref, w3_ref, w2_ref):
    h = (x * _rms_scale(x) * (norm_g * (1.0 + scale)) + shift).astype(BF16)
    a = jnp.dot(h, w1_ref[...], preferred_element_type=F32)
    b = jnp.dot(h, w3_ref[...], preferred_element_type=F32)
    g = (a * jax.nn.sigmoid(a) * b).astype(BF16)
    o = jnp.dot(g, w2_ref[...], preferred_element_type=F32)
    return x + (0.5 * gate) * o


def _ffn1_kernel(x_ref, mod_ref, ng_ref, w1_ref, w3_ref, w2_ref, win_ref, *refs):
    n_cast = (len(refs) - 3) // 2
    o_ref, wmain_ref, wtail_ref = refs[n_cast:n_cast + 3]
    mod = mod_ref[0]
    for r in range(0, x_ref.shape[0], ROW_TILE):
        rows = slice(r, r + ROW_TILE)
        o_ref[rows, :] = _ffn_update(x_ref[rows, :], mod[0:1], mod[1:2], mod[2:3], ng_ref[...],
                                     w1_ref, w3_ref, w2_ref)
    _run_casts(refs[:n_cast], refs[n_cast + 3:])
    w = win_ref[...]
    n_main = wmain_ref.shape[1]
    kr0 = w.shape[1] - QK_ROPE
    half = QK_ROPE // 2
    wmain_ref[...] = w[:, :n_main].astype(BF16)
    wtail_ref[...] = jnp.concatenate(
        [w[:, n_main:], w[:, kr0 + half:], w[:, kr0:kr0 + half]], axis=1).astype(BF16)


def _ffn1(x2d, mod, norm_g, w1, w3, w2, tiles_per_batch, w_in, cast_weights):
    t, d = x2d.shape
    tm = FFN_STEP_ROWS
    tiles_per_batch = tiles_per_batch * ROW_TILE // tm
    n_steps = t // tm
    c_in, c_out, c_shapes = _cast_streams(cast_weights, n_steps)
    rows, in_cols = w_in.shape
    n_main = (in_cols // MXU_TILE) * MXU_TILE
    assert in_cols - n_main + QK_ROPE == MXU_TILE and rows % n_steps == 0
    wr = rows // n_steps
    outs = pl.pallas_call(
        _ffn1_kernel,
        grid=(n_steps,),
        in_specs=[
            pl.BlockSpec((tm, d), lambda i: (i, 0)),
            pl.BlockSpec((1, N_MOD, d), lambda i: (i // tiles_per_batch, 0, 0)),
            _resident((1, d)),
            _resident(w1.shape), _resident(w3.shape), _resident(w2.shape),
            pl.BlockSpec((wr, in_cols), lambda i: (i, 0)),
        ] + c_in,
        out_specs=[pl.BlockSpec((tm, d), lambda i: (i, 0)),
                   pl.BlockSpec((wr, n_main), lambda i: (i, 0)),
                   pl.BlockSpec((wr, MXU_TILE), lambda i: (i, 0))] + c_out,
        out_shape=[jax.ShapeDtypeStruct((t, d), F32),
                   jax.ShapeDtypeStruct((rows, n_main), BF16),
                   jax.ShapeDtypeStruct((rows, MXU_TILE), BF16)] + c_shapes,
        compiler_params=_params(),
        name="ffn1",
    )(x2d, mod, norm_g, w1, w3, w2, w_in, *cast_weights)
    return outs[0], outs[1], outs[2], outs[3:]


def _proj_kernel(x_ref, mod_ref, ng_ref, win_ref, wtail_ref, cw_ref, gna_ref, qng_ref, wq_ref, kvng_ref,
                 wk_ref, wv_ref, cs_ref, cst_ref,
                 ya_ref, q_ref, k_ref, v_ref, zbuf_a, zbuf_b, ubuf, *, tiles_per_batch):
    i = pl.program_id(0)

    @pl.when(i == 0)
    def _():
        zbuf_b[...] = jnp.zeros(zbuf_b.shape, F32)
        ubuf[0:SUBLANES, :] = jnp.zeros((SUBLANES, CONV_WIDTH), F32)

    def step(z_new, z_old):
        x = x_ref[...]
        mod = mod_ref[0]
        h = (x * _rms_scale(x) * (ng_ref[...] * (1.0 + mod[4:5])) + mod[3:4]).astype(BF16)
        post = _mixer_post_pieces(z_old, i - 1, cw_ref, gna_ref, qng_ref, wq_ref, kvng_ref, wk_ref,
                                  wv_ref, cs_ref, cst_ref, ya_ref, q_ref, k_ref, v_ref, ubuf,
                                  tiles_per_batch)
        n_main = win_ref.shape[1] // MXU_TILE
        assert wtail_ref.shape[1] == MXU_TILE and len(post) == n_main + 1
        for c in range(n_main + 1):
            cols = slice(c * MXU_TILE, (c + 1) * MXU_TILE)
            w_c = win_ref[:, cols] if c < n_main else wtail_ref[...]
            z_new[:, cols] = jnp.dot(h, w_c, preferred_element_type=F32)
            post[c]()

    pl.when(i % 2 == 0)(functools.partial(step, zbuf_a, zbuf_b))
    pl.when(i % 2 == 1)(functools.partial(step, zbuf_b, zbuf_a))


def _mixer_post_pieces(z, tile, cw_ref, gna_ref, qng_ref, wq_ref, kvng_ref, wk_ref, wv_ref, cs_ref,
                       cst_ref, ya_ref, q_ref, k_ref, v_ref, ubuf, tiles_per_batch):
    tm = z.shape[0]
    cwid = CONV_WIDTH
    c0 = 3 * cwid
    nt = (((1,), (1,)), ((), ()))
    scale = (QK_NOPE + QK_ROPE) ** -0.5 * math.log2(math.e)
    state = {}

    def q_head(hh):
        if hh == 0:
            cq = z[:, c0:c0 + Q_LORA]
            state["cqn"] = (cq * _rms_scale(cq) * qng_ref[...]).astype(BF16)
        base = hh * QK_PAD
        qt = lax.dot_general(wq_ref[base:base + QK_PAD, :], state["cqn"], nt,
                             preferred_element_type=F32)
        ct = cst_ref[0:QK_ROPE, :]
        st = cst_ref[QK_ROPE:2 * QK_ROPE, :]
        rope = qt[QK_NOPE:QK_NOPE + QK_ROPE] * ct + qt[QK_NOPE + QK_ROPE:QK_PAD] * st
        q_ref[0, base:base + QK_PAD, :] = jnp.concatenate(
            [qt[0:QK_NOPE] * scale, rope * scale, jnp.zeros((QK_ROPE, tm), F32)],
            axis=0).astype(q_ref.dtype)

    def kv():
        ckv = z[:, c0 + Q_LORA:c0 + Q_LORA + KV_LORA]
        ckvn = (ckv * _rms_scale(ckv) * kvng_ref[...]).astype(BF16)
        kn = jnp.dot(ckvn, wk_ref[...], preferred_element_type=F32)
        krr = z[:, c0 + Q_LORA + KV_LORA:c0 + Q_LORA + KV_LORA + LANES]
        a = krr * cs_ref[...]
        kr = a + pltpu.roll(a, QK_ROPE, axis=1)
        kp = []
        for hh in range(MLA_HEADS):
            kp += [kn[:, hh * QK_NOPE:(hh + 1) * QK_NOPE], kr]
        k_ref[...] = jnp.concatenate(kp, axis=1).astype(k_ref.dtype)
        vt = lax.dot_general(wv_ref[...], ckvn, nt, preferred_element_type=F32)
        ones = jnp.ones((V_ROWS - V_HEAD, tm), F32)
        vp = []
        for hh in range(MLA_HEADS):
            vp += [vt[hh * V_HEAD:(hh + 1) * V_HEAD], ones]
        v_ref[0] = jnp.concatenate(vp, axis=0).astype(v_ref.dtype)

    def conv_tile(j):
        cols = slice(j * LANES, (j + 1) * LANES)
        ubuf[0:SUBLANES, cols] = jnp.where(tile % tiles_per_batch == 0, 0.0, ubuf[0:SUBLANES, cols])
        u = z[:, cwid + j * LANES:cwid + (j + 1) * LANES] * z[:, 2 * cwid + j * LANES:2 * cwid + (j + 1) * LANES]
        ubuf[SUBLANES:SUBLANES + tm, cols] = u
        u1 = ubuf[SUBLANES - 1:SUBLANES - 1 + tm, cols]
        u2 = ubuf[SUBLANES - 2:SUBLANES - 2 + tm, cols]
        cw = cw_ref[:, cols]
        ya = z[:, cols] * (cw[0:1] * u2 + cw[1:2] * u1 + cw[2:3] * u)
        ubuf[0:SUBLANES, cols] = ubuf[tm:tm + SUBLANES, cols]
        gw = cwid // CONV_GROUPS
        low = lax.broadcasted_iota(jnp.int32, (tm, LANES), 1) < gw
        sq = ya * ya
        s_lo = jnp.sum(jnp.where(low, sq, 0.0), axis=-1, keepdims=True)
        s_hi = jnp.sum(jnp.where(low, 0.0, sq), axis=-1, keepdims=True)
        ms = jnp.where(low, s_lo, s_hi) * (1.0 / gw)
        ya_ref[:, cols] = (ya * lax.rsqrt(ms + EPS) * gna_ref[:, cols]).astype(ya_ref.dtype)

    pieces = [functools.partial(q_head, hh) for hh in range(MLA_HEADS)] + [kv]
    pieces += [functools.partial(conv_tile, j) for j in range(cwid // LANES)]
    return pieces


def _proj(x1, mod, norm_g, w_main, w_tail, conv_w, gn_a, q_norm_g, wq_t, kv_norm_g, wk, wv_t, cs_row, cs_t,
          tiles_per_batch):
    t, d = x1.shape
    tm = ROW_TILE
    nt = t // tm
    hq = MLA_HEADS * QK_PAD

    def cur(i):
        return jnp.minimum(i, nt - 1)

    def prev(i):
        return jnp.maximum(i - 1, 0)

    return pl.pallas_call(
        functools.partial(_proj_kernel, tiles_per_batch=tiles_per_batch),
        grid=(nt + 1,),
        in_specs=[
            pl.BlockSpec((tm, d), lambda i: (cur(i), 0)),
            pl.BlockSpec((1, N_MOD, d), lambda i: (cur(i) // tiles_per_batch, 0, 0)),
            _resident((1, d)),
            _resident(w_main.shape),
            _resident(w_tail.shape),
            _resident(conv_w.shape),
            _resident(gn_a.shape),
            _resident(q_norm_g.shape),
            _resident(wq_t.shape),
            _resident(kv_norm_g.shape),
            _resident(wk.shape),
            _resident(wv_t.shape),
            pl.BlockSpec((tm, LANES), lambda i: (prev(i), 0)),
            pl.BlockSpec((LANES, tm), lambda i: (0, prev(i))),
        ],
        out_specs=[
            pl.BlockSpec((tm, CONV_WIDTH), lambda i: (prev(i), 0)),
            pl.BlockSpec((1, hq, tm), lambda i: (prev(i), 0, 0)),
            pl.BlockSpec((tm, hq), lambda i: (prev(i), 0)),
            pl.BlockSpec((1, MLA_HEADS * V_ROWS, tm), lambda i: (prev(i), 0, 0)),
        ],
        out_shape=[
            jax.ShapeDtypeStruct((t, CONV_WIDTH), BF16),
            jax.ShapeDtypeStruct((nt, hq, tm), BF16),
            jax.ShapeDtypeStruct((t, hq), BF16),
            jax.ShapeDtypeStruct((nt, MLA_HEADS * V_ROWS, tm), BF16),
        ],
        scratch_shapes=[pltpu.VMEM((tm, w_main.shape[1] + w_tail.shape[1]), F32),
                        pltpu.VMEM((tm, w_main.shape[1] + w_tail.shape[1]), F32),
                        pltpu.VMEM((tm + SUBLANES, CONV_WIDTH), F32)],
        compiler_params=_params(),
        name="mixer_proj",
    )(x1, mod, norm_g, w_main, w_tail, conv_w, gn_a, q_norm_g, wq_t, kv_norm_g, wk, wv_t, cs_row, cs_t)


def _attn_kernel(q_ref, k_hbm, v_hbm, g_ref, o_ref, k_ref, v_ref, kv_sem, s_sc, m_sc, acc_sc):
    b = pl.program_id(0)
    i = pl.program_id(1)
    tq = q_ref.shape[2]
    nblk, _, tk = v_ref.shape

    def k_copy(j):
        return pltpu.make_async_copy(k_hbm.at[pl.ds((b * nblk + j) * tk, tk), :],
                                     k_ref.at[pl.ds(j * tk, tk), :], kv_sem.at[0, j])

    def v_copy(j):
        return pltpu.make_async_copy(v_hbm.at[b * nblk + j], v_ref.at[j], kv_sem.at[1, j])

    @pl.when(i == 0)
    def _():
        for j in range(nblk):
            k_copy(j).start()
            v_copy(j).start()

    k_copy(i).wait()
    v_copy(i).wait()

    m_sc[...] = jnp.full(m_sc.shape, -jnp.inf, F32)
    acc_sc[...] = jnp.zeros(acc_sc.shape, F32)

    def scores(j, hh, slot):
        row0 = pl.multiple_of(j * tk, tk)
        s_sc[slot] = jnp.dot(k_ref[pl.ds(row0, tk), hh * QK_PAD:(hh + 1) * QK_PAD],
                             q_ref[0, hh * QK_PAD:(hh + 1) * QK_PAD, :],
                             preferred_element_type=F32)

    def softmax_pv(j, hh, slot, masked):
        s = s_sc[slot]
        if masked:
            kc = lax.broadcasted_iota(jnp.int32, (tk, tq), 0) // CHUNK
            qc = lax.broadcasted_iota(jnp.int32, (tk, tq), 1) // CHUNK
            s = jnp.where(kc <= qc, s, -1e30)
        m_prev = m_sc[hh]
        m_new = jnp.maximum(m_prev, jnp.max(s, axis=0, keepdims=True))
        alpha = jnp.exp2(m_prev - m_new)
        p = jnp.exp2(s - m_new).astype(BF16)
        pv = jnp.dot(v_ref[j, hh * V_ROWS:(hh + 1) * V_ROWS, :], p,
                     preferred_element_type=F32)
        acc_sc[hh] = alpha * acc_sc[hh] + pv
        m_sc[hh] = m_new

    scores(0, 0, 0)

    def block(j, masked, next_j):
        for hh in range(MLA_HEADS):
            if hh + 1 < MLA_HEADS:
                scores(j, hh + 1, (hh + 1) % 2)
            elif next_j is not None:
                scores(next_j, 0, 0)
            softmax_pv(j, hh, hh % 2, masked)

    def run(j0, n):
        for t in range(n):
            block(j0 + t, False, j0 + t + 1)

    n_quads = lax.shift_right_logical(i, 2)

    def quad(q, carry):
        run(q * ATTN_UNROLL, ATTN_UNROLL)
        return carry

    lax.fori_loop(0, n_quads, quad, 0)
    done = n_quads * ATTN_UNROLL

    @pl.when((i & 2) != 0)
    def _():
        run(done, 2)

    @pl.when((i & 1) != 0)
    def _():
        run(done + (i & 2), 1)

    block(i, True, None)

    g = g_ref[...]
    for hh in range(MLA_HEADS):
        acc = acc_sc[hh]
        o_t = acc[0:V_HEAD] / acc[V_HEAD:V_HEAD + 1]
        o_t = o_t * lax.rsqrt(jnp.mean(o_t * o_t, axis=0, keepdims=True) + EPS)
        o_ref[:, hh * V_HEAD:(hh + 1) * V_HEAD] = (
            o_t.T * g[:, hh * V_HEAD:(hh + 1) * V_HEAD]).astype(o_ref.dtype)


def _attention(q_t, k, v_t, gn_b, bsz, seq):
    tq = ATTN_TQ
    tk = v_t.shape[2]
    assert tq == tk
    nq = seq // tq
    hq = MLA_HEADS * QK_PAD
    nblk = seq // tk
    return pl.pallas_call(
        _attn_kernel,
        grid=(bsz, nq),
        in_specs=[
            pl.BlockSpec((1, hq, tq), lambda b, i: (b * nq + i, 0, 0)),
            pl.BlockSpec(memory_space=pl.ANY),
            pl.BlockSpec(memory_space=pl.ANY),
            _resident(gn_b.shape),
        ],
        out_specs=pl.BlockSpec((tq, MLA_WIDTH), lambda b, i: (b * nq + i, 0)),
        out_shape=jax.ShapeDtypeStruct((bsz * seq, MLA_WIDTH), BF16),
        scratch_shapes=[
            pltpu.VMEM((seq, hq), BF16),
            pltpu.VMEM((nblk, MLA_HEADS * V_ROWS, tk), BF16),
            pltpu.SemaphoreType.DMA((2, nblk)),
            pltpu.VMEM((2, tk, tq), F32),
            pltpu.VMEM((MLA_HEADS, 1, tq), F32),
            pltpu.VMEM((MLA_HEADS, V_ROWS, tq), F32),
        ],
        compiler_params=_params(2),
        name="mla_attention",
    )(q_t, k, v_t, gn_b)


def _out_kernel(x_ref, ya_ref, yb_ref, mod_ref, wo_ref, ng_ref, w1_ref, w3_ref, w2_ref, fg_ref, o_ref):
    mod = mod_ref[0]
    for r in range(0, x_ref.shape[0], ROW_TILE):
        rows = slice(r, r + ROW_TILE)
        y = jnp.concatenate([ya_ref[rows, :], yb_ref[rows, :]], axis=1)
        x = x_ref[rows, :] + mod[5:6] * jnp.dot(y, wo_ref[...], preferred_element_type=F32)
        x = _ffn_update(x, mod[6:7], mod[7:8], mod[8:9], ng_ref[...], w1_ref, w3_ref, w2_ref)
        o_ref[rows, :] = x * _rms_scale(x) * fg_ref[...]


def _out_ffn2(x1, ya, yb, mod, w_out, norm_g, w1, w3, w2, final_g, tiles_per_batch):
    t, d = x1.shape
    tm = FFN_STEP_ROWS
    tiles_per_batch = tiles_per_batch * ROW_TILE // tm
    return pl.pallas_call(
        _out_kernel,
        grid=(t // tm,),
        in_specs=[
            pl.BlockSpec((tm, d), lambda i: (i, 0)),
            pl.BlockSpec((tm, CONV_WIDTH), lambda i: (i, 0)),
            pl.BlockSpec((tm, MLA_WIDTH), lambda i: (i, 0)),
            pl.BlockSpec((1, N_MOD, d), lambda i: (i // tiles_per_batch, 0, 0)),
            _resident(w_out.shape),
            _resident((1, d)),
            _resident(w1.shape), _resident(w3.shape), _resident(w2.shape),
            _resident((1, d)),
        ],
        out_specs=pl.BlockSpec((tm, d), lambda i: (i, 0)),
        out_shape=jax.ShapeDtypeStruct((t, d), F32),
        compiler_params=_params(),
        name="out_ffn2",
    )(x1, ya, yb, mod, w_out, norm_g, w1, w3, w2, final_g)


def _swap_halves(w, axis):
    lo, hi = jnp.split(w, 2, axis=axis)
    return jnp.concatenate([hi, lo], axis=axis)


def kernel(x, c, positions, ada_w, ada_b, norm_ffn1_g, ffn1_w1, ffn1_w3, ffn1_w2, norm_mix_g, w_in, conv_w, q_norm_g, w_uq, kv_norm_g, w_ukv, out_norm_g, w_out, norm_ffn2_g, ffn2_w1, ffn2_w3, ffn2_w2, final_norm_g):
    bsz, seq, d = x.shape
    depth = ada_w.shape[0]
    t = bsz * seq
    tiles_per_batch = seq // ROW_TILE
    half = QK_ROPE // 2

    assert depth == 1, "only the one-layer block is implemented"
    l = 0

    cs_t, cs_row, mod, (f1w1, f1w3, f1w2) = _prologue(
        positions, c, ada_w[l], ada_b[l], [ffn1_w1[l], ffn1_w3[l], ffn1_w2[l]])
    mod = mod.reshape(bsz, N_MOD, d)

    xf = x.reshape(t, d)

    wq = w_uq[l].T.reshape(MLA_HEADS, QK_NOPE + QK_ROPE, Q_LORA)
    wq_rope = wq[:, QK_NOPE:]
    wq_t = jnp.concatenate([wq, _swap_halves(wq_rope, 1)], axis=1).reshape(
        MLA_HEADS * QK_PAD, Q_LORA).astype(BF16)
    wkv = w_ukv[l].reshape(KV_LORA, MLA_HEADS, QK_NOPE + V_HEAD)
    wk = wkv[:, :, :QK_NOPE].reshape(KV_LORA, MLA_HEADS * QK_NOPE).astype(BF16)
    wv_t = wkv[:, :, QK_NOPE:].reshape(KV_LORA, MLA_WIDTH).T.astype(BF16)

    x1, w_main, w_tail, (wo, f2w1, f2w3, f2w2) = _ffn1(
        xf, mod, norm_ffn1_g[l].reshape(1, d), f1w1, f1w3, f1w2, tiles_per_batch,
        w_in[l], [w_out[l], ffn2_w1[l], ffn2_w3[l], ffn2_w2[l]])
    ya, q_t, k, v_t = _proj(
        x1, mod, norm_mix_g[l].reshape(1, d), w_main, w_tail, conv_w[l],
        out_norm_g[l, :CONV_WIDTH].reshape(1, CONV_WIDTH), q_norm_g[l].reshape(1, Q_LORA), wq_t,
        kv_norm_g[l].reshape(1, KV_LORA), wk, wv_t, cs_row, cs_t, tiles_per_batch)
    yb = _attention(q_t, k, v_t, out_norm_g[l, CONV_WIDTH:].reshape(1, MLA_WIDTH), bsz, seq)
    xf = _out_ffn2(x1, ya, yb, mod, wo, norm_ffn2_g[l].reshape(1, d), f2w1, f2w3, f2w2,
                   final_norm_g.reshape(1, d), tiles_per_batch)
    return xf.reshape(bsz, seq, d)
```

```python
import functools
import math

import jax
import jax.numpy as jnp
from jax import lax
from jax.experimental import pallas as pl
from jax.experimental.pallas import tpu as pltpu

CHUNK = 64
EPS = 1e-6
N_MOD = 9
CONV_WIDTH = 512
CONV_GROUPS = 8
CONV_K = 3
MLA_HEADS = 4
QK_NOPE = 128
QK_ROPE = 64
V_HEAD = 128
Q_LORA = 384
KV_LORA = 256
ROPE_THETA = 10000.0
MLA_WIDTH = MLA_HEADS * V_HEAD

LANES = 128
SUBLANES = 8
MXU_TILE = 256
QK_PAD = MXU_TILE
BF16_ROWS = 16
V_ROWS = V_HEAD + BF16_ROWS
VMEM_LIMIT_BYTES = 56 * 1024 * 1024

ROW_TILE = 512
FFN_STEP_ROWS = 1024
ATTN_TQ = 512
ATTN_UNROLL = 4
PROLOGUE_STEPS = 8

F32 = jnp.float32
BF16 = jnp.bfloat16


def _params(n_axes=1, flags=None):
    return pltpu.CompilerParams(
        dimension_semantics=("arbitrary",) * n_axes,
        vmem_limit_bytes=VMEM_LIMIT_BYTES,
        flags=flags,
    )


def _resident(shape):
    nd = len(shape)
    return pl.BlockSpec(shape, lambda *_: (0,) * nd, pipeline_mode=pl.Buffered(1))


def _rms_scale(x):
    return lax.rsqrt(jnp.mean(x * x, axis=-1, keepdims=True) + EPS)


def _cast_streams(weights, n_steps):
    in_specs, out_specs, out_shapes = [], [], []
    for w in weights:
        rows, width = w.shape
        s = max(k for k in range(1, n_steps + 1)
                if rows % k == 0 and (rows // k) % BF16_ROWS == 0)
        spec = pl.BlockSpec((rows // s, width), lambda i, s=s: (jnp.minimum(i, s - 1), 0))
        in_specs.append(spec)
        out_specs.append(spec)
        out_shapes.append(jax.ShapeDtypeStruct((rows, width), BF16))
    return in_specs, out_specs, out_shapes


def _run_casts(src_refs, dst_refs):
    for src, dst in zip(src_refs, dst_refs):
        dst[...] = src[...].astype(dst.dtype)


def _prologue_kernel(pos_ref, inv_ref, cb_ref, w_ref, b_ref, *refs):
    n_cast = (len(refs) - 3) // 2
    cst_ref, csr_ref, mod_ref = refs[n_cast:n_cast + 3]

    ang = pos_ref[...].astype(F32) * inv_ref[...]
    c = jnp.cos(ang)
    s = jnp.sin(ang)
    cst = jnp.concatenate([c, c, -s, s], axis=0)
    cst_ref[...] = cst
    csr_ref[...] = cst.T

    w = w_ref[...]
    tn = w.shape[1]
    for b in range(cb_ref.shape[0]):
        cb = cb_ref[b]
        act = cb * jax.nn.sigmoid(cb)
        cols = [jnp.sum(w[:, j * LANES:(j + 1) * LANES] * act, axis=0, keepdims=True)
                for j in range(tn // LANES)]
        mod_ref[b:b + 1, :] = jnp.concatenate(cols, axis=1) + b_ref[...]

    _run_casts(refs[:n_cast], refs[n_cast + 3:])


def _prologue(positions, c, ada_w, ada_b, cast_weights):
    half = QK_ROPE // 2
    t = positions.size
    n_steps = PROLOGUE_STEPS
    tc = t // n_steps
    bsz, d = c.shape
    n = ada_w.shape[1]
    tn = n // n_steps
    assert t % n_steps == 0 and n % n_steps == 0 and tn % LANES == 0 and tc % LANES == 0
    inv_freq = ROPE_THETA ** (-jnp.arange(0, QK_ROPE, 2, dtype=F32) / QK_ROPE)
    cb = jnp.broadcast_to(c[:, :, None], (bsz, d, LANES))
    c_in, c_out, c_shapes = _cast_streams(cast_weights, n_steps)
    outs = pl.pallas_call(
        _prologue_kernel,
        grid=(n_steps,),
        in_specs=[pl.BlockSpec((1, tc), lambda i: (0, i)),
                  pl.BlockSpec((half, 1), lambda i: (0, 0)),
                  pl.BlockSpec((bsz, d, LANES), lambda i: (0, 0, 0)),
                  pl.BlockSpec((d, tn), lambda i: (0, i)),
                  pl.BlockSpec((1, tn), lambda i: (0, i))] + c_in,
        out_specs=[pl.BlockSpec((4 * half, tc), lambda i: (0, i)),
                   pl.BlockSpec((tc, 4 * half), lambda i: (i, 0)),
                   pl.BlockSpec((bsz, tn), lambda i: (0, i))] + c_out,
        out_shape=[jax.ShapeDtypeStruct((4 * half, t), F32),
                   jax.ShapeDtypeStruct((t, 4 * half), F32),
                   jax.ShapeDtypeStruct((bsz, n), F32)] + c_shapes,
        compiler_params=_params(),
        name="prologue",
    )(positions.reshape(1, t), inv_freq.reshape(half, 1), cb, ada_w, ada_b.reshape(1, n), *cast_weights)
    return outs[0], outs[1], outs[2], outs[3:]


def _ffn_update(x, shift, scale, gate, norm_g, w1_ref, w3_ref, w2_ref):
    h = (x * _rms_scale(x) * (norm_g * (1.0 + scale)) + shift).astype(BF16)
    a = jnp.dot(h, w1_ref[...], preferred_element_type=F32)
    b = jnp.dot(h, w3_ref[...], preferred_element_type=F32)
    g = (a * jax.nn.sigmoid(a) * b).astype(BF16)
    o = jnp.dot(g, w2_ref[...], preferred_element_type=F32)
    return x + (0.5 * gate) * o


def _ffn1_kernel(x_ref, mod_ref, ng_ref, w1_ref, w3_ref, w2_ref, *refs):
    n_cast = (len(refs) - 1) // 2
    o_ref = refs[n_cast]
    mod = mod_ref[0]
    for r in range(0, x_ref.shape[0], ROW_TILE):
        rows = slice(r, r + ROW_TILE)
        o_ref[rows, :] = _ffn_update(x_ref[rows, :], mod[0:1], mod[1:2], mod[2:3], ng_ref[...],
                                     w1_ref, w3_ref, w2_ref)
    _run_casts(refs[:n_cast], refs[n_cast + 1:])


def _ffn1(x2d, mod, norm_g, w1, w3, w2, tiles_per_batch, cast_weights):
    t, d = x2d.shape
    tm = FFN_STEP_ROWS
    tiles_per_batch = tiles_per_batch * ROW_TILE // tm
    n_steps = t // tm
    c_in, c_out, c_shapes = _cast_streams(cast_weights, n_steps)
    outs = pl.pallas_call(
        _ffn1_kernel,
        grid=(n_steps,),
        in_specs=[
            pl.BlockSpec((tm, d), lambda i: (i, 0)),
            pl.BlockSpec((1, N_MOD, d), lambda i: (i // tiles_per_batch, 0, 0)),
            _resident((1, d)),
            _resident(w1.shape), _resident(w3.shape), _resident(w2.shape),
        ] + c_in,
        out_specs=[pl.BlockSpec((tm, d), lambda i: (i, 0))] + c_out,
        out_shape=[jax.ShapeDtypeStruct((t, d), F32)] + c_shapes,
        compiler_params=_params(),
        name="ffn1",
    )(x2d, mod, norm_g, w1, w3, w2, *cast_weights)
    return outs[0], outs[1:]


def _proj_kernel(x_ref, mod_ref, ng_ref, wint_ref, cw_ref, gna_ref, qng_ref, wq_ref, kvng_ref,
                 wk_ref, wv_ref, cs_ref, cst_ref,
                 ya_ref, q_ref, k_ref, v_ref, zbuf_a, zbuf_b, ubuf, *, tiles_per_batch):
    i = pl.program_id(0)

    @pl.when(i == 0)
    def _():
        zbuf_b[...] = jnp.zeros(zbuf_b.shape, F32)
        ubuf[0:SUBLANES, :] = jnp.zeros((SUBLANES, CONV_WIDTH), F32)

    def step(z_new, z_old):
        x = x_ref[...]
        mod = mod_ref[0]
        h = (x * _rms_scale(x) * (ng_ref[...] * (1.0 + mod[4:5])) + mod[3:4]).astype(BF16)
        post = _mixer_post_pieces(z_old, i - 1, cw_ref, gna_ref, qng_ref, wq_ref, kvng_ref, wk_ref,
                                  wv_ref, cs_ref, cst_ref, ya_ref, q_ref, k_ref, v_ref, ubuf,
                                  tiles_per_batch)
        n_out = wint_ref.shape[0]
        n_main = n_out // MXU_TILE
        half = QK_ROPE // 2
        assert n_out - n_main * MXU_TILE + QK_ROPE == MXU_TILE and len(post) == n_main + 1
        nt = (((1,), (1,)), ((), ()))
        for c in range(n_main + 1):
            cols = slice(c * MXU_TILE, (c + 1) * MXU_TILE)
            if c < n_main:
                w_c = wint_ref[cols, :]
            else:
                w_c = jnp.concatenate([wint_ref[n_main * MXU_TILE:, :], wint_ref[n_out - half:, :],
                                       wint_ref[n_out - QK_ROPE:n_out - half, :]], axis=0)
            z_new[:, cols] = lax.dot_general(h, w_c, nt, preferred_element_type=F32)
            post[c]()

    pl.when(i % 2 == 0)(functools.partial(step, zbuf_a, zbuf_b))
    pl.when(i % 2 == 1)(functools.partial(step, zbuf_b, zbuf_a))


def _mixer_post_pieces(z, tile, cw_ref, gna_ref, qng_ref, wq_ref, kvng_ref, wk_ref, wv_ref, cs_ref,
                       cst_ref, ya_ref, q_ref, k_ref, v_ref, ubuf, tiles_per_batch):
    tm = z.shape[0]
    cwid = CONV_WIDTH
    c0 = 3 * cwid
    nt = (((1,), (1,)), ((), ()))
    scale = (QK_NOPE + QK_ROPE) ** -0.5 * math.log2(math.e)
    state = {}

    def q_head(hh):
        if hh == 0:
            cq = z[:, c0:c0 + Q_LORA]
            state["cqn"] = (cq * _rms_scale(cq) * qng_ref[...]).astype(BF16)
        base = hh * QK_PAD
        qt = lax.dot_general(wq_ref[base:base + QK_PAD, :], state["cqn"], nt,
                             preferred_element_type=F32)
        ct = cst_ref[0:QK_ROPE, :]
        st = cst_ref[QK_ROPE:2 * QK_ROPE, :]
        rope = qt[QK_NOPE:QK_NOPE + QK_ROPE] * ct + qt[QK_NOPE + QK_ROPE:QK_PAD] * st
        q_ref[0, base:base + QK_PAD, :] = jnp.concatenate(
            [qt[0:QK_NOPE] * scale, rope * scale, jnp.zeros((QK_ROPE, tm), F32)],
            axis=0).astype(q_ref.dtype)

    def kv():
        ckv = z[:, c0 + Q_LORA:c0 + Q_LORA + KV_LORA]
        ckvn = (ckv * _rms_scale(ckv) * kvng_ref[...]).astype(BF16)
        kn = jnp.dot(ckvn, wk_ref[...], preferred_element_type=F32)
        krr = z[:, c0 + Q_LORA + KV_LORA:c0 + Q_LORA + KV_LORA + LANES]
        a = krr * cs_ref[...]
        kr = a + pltpu.roll(a, QK_ROPE, axis=1)
        kp = []
        for hh in range(MLA_HEADS):
            kp += [kn[:, hh * QK_NOPE:(hh + 1) * QK_NOPE], kr]
        k_ref[...] = jnp.concatenate(kp, axis=1).astype(k_ref.dtype)
        vt = lax.dot_general(wv_ref[...], ckvn, nt, preferred_element_type=F32)
        ones = jnp.ones((V_ROWS - V_HEAD, tm), F32)
        vp = []
        for hh in range(MLA_HEADS):
            vp += [vt[hh * V_HEAD:(hh + 1) * V_HEAD], ones]
        v_ref[0] = jnp.concatenate(vp, axis=0).astype(v_ref.dtype)

    def conv_tile(j):
        cols = slice(j * LANES, (j + 1) * LANES)
        ubuf[0:SUBLANES, cols] = jnp.where(tile % tiles_per_batch == 0, 0.0, ubuf[0:SUBLANES, cols])
        u = z[:, cwid + j * LANES:cwid + (j + 1) * LANES] * z[:, 2 * cwid + j * LANES:2 * cwid + (j + 1) * LANES]
        ubuf[SUBLANES:SUBLANES + tm, cols] = u
        u1 = ubuf[SUBLANES - 1:SUBLANES - 1 + tm, cols]
        u2 = ubuf[SUBLANES - 2:SUBLANES - 2 + tm, cols]
        cw = cw_ref[:, cols]
        ya = z[:, cols] * (cw[0:1] * u2 + cw[1:2] * u1 + cw[2:3] * u)
        ubuf[0:SUBLANES, cols] = ubuf[tm:tm + SUBLANES, cols]
        gw = cwid // CONV_GROUPS
        low = lax.broadcasted_iota(jnp.int32, (tm, LANES), 1) < gw
        sq = ya * ya
        s_lo = jnp.sum(jnp.where(low, sq, 0.0), axis=-1, keepdims=True)
        s_hi = jnp.sum(jnp.where(low, 0.0, sq), axis=-1, keepdims=True)
        ms = jnp.where(low, s_lo, s_hi) * (1.0 / gw)
        ya_ref[:, cols] = (ya * lax.rsqrt(ms + EPS) * gna_ref[:, cols]).astype(ya_ref.dtype)

    pieces = [functools.partial(q_head, hh) for hh in range(MLA_HEADS)] + [kv]
    pieces += [functools.partial(conv_tile, j) for j in range(cwid // LANES)]
    return pieces


def _proj(x1, mod, norm_g, w_in_t, conv_w, gn_a, q_norm_g, wq_t, kv_norm_g, wk, wv_t, cs_row, cs_t,
          tiles_per_batch):
    t, d = x1.shape
    tm = ROW_TILE
    nt = t // tm
    hq = MLA_HEADS * QK_PAD

    def cur(i):
        return jnp.minimum(i, nt - 1)

    def prev(i):
        return jnp.maximum(i - 1, 0)

    return pl.pallas_call(
        functools.partial(_proj_kernel, tiles_per_batch=tiles_per_batch),
        grid=(nt + 1,),
        in_specs=[
            pl.BlockSpec((tm, d), lambda i: (cur(i), 0)),
            pl.BlockSpec((1, N_MOD, d), lambda i: (cur(i) // tiles_per_batch, 0, 0)),
            _resident((1, d)),
            _resident(w_in_t.shape),
            _resident(conv_w.shape),
            _resident(gn_a.shape),
            _resident(q_norm_g.shape),
            _resident(wq_t.shape),
            _resident(kv_norm_g.shape),
            _resident(wk.shape),
            _resident(wv_t.shape),
            pl.BlockSpec((tm, LANES), lambda i: (prev(i), 0)),
            pl.BlockSpec((LANES, tm), lambda i: (0, prev(i))),
        ],
        out_specs=[
            pl.BlockSpec((tm, CONV_WIDTH), lambda i: (prev(i), 0)),
            pl.BlockSpec((1, hq, tm), lambda i: (prev(i), 0, 0)),
            pl.BlockSpec((tm, hq), lambda i: (prev(i), 0)),
            pl.BlockSpec((1, MLA_HEADS * V_ROWS, tm), lambda i: (prev(i), 0, 0)),
        ],
        out_shape=[
            jax.ShapeDtypeStruct((t, CONV_WIDTH), BF16),
            jax.ShapeDtypeStruct((nt, hq, tm), BF16),
            jax.ShapeDtypeStruct((t, hq), BF16),
            jax.ShapeDtypeStruct((nt, MLA_HEADS * V_ROWS, tm), BF16),
        ],
        scratch_shapes=[pltpu.VMEM((tm, w_in_t.shape[0] + QK_ROPE), F32),
                        pltpu.VMEM((tm, w_in_t.shape[0] + QK_ROPE), F32),
                        pltpu.VMEM((tm + SUBLANES, CONV_WIDTH), F32)],
        compiler_params=_params(),
        name="mixer_proj",
    )(x1, mod, norm_g, w_in_t, conv_w, gn_a, q_norm_g, wq_t, kv_norm_g, wk, wv_t, cs_row, cs_t)


def _attn_kernel(q_ref, k_hbm, v_hbm, g_ref, o_ref, k_ref, v_ref, kv_sem, s_sc, m_sc, acc_sc):
    b = pl.program_id(0)
    i = pl.program_id(1)
    tq = q_ref.shape[2]
    nblk, _, tk = v_ref.shape

    def k_copy(j):
        return pltpu.make_async_copy(k_hbm.at[pl.ds((b * nblk + j) * tk, tk), :],
                                     k_ref.at[pl.ds(j * tk, tk), :], kv_sem.at[0, j])

    def v_copy(j):
        return pltpu.make_async_copy(v_hbm.at[b * nblk + j], v_ref.at[j], kv_sem.at[1, j])

    @pl.when(i == 0)
    def _():
        for j in range(nblk):
            k_copy(j).start()
            v_copy(j).start()

    k_copy(i).wait()
    v_copy(i).wait()

    m_sc[...] = jnp.full(m_sc.shape, -jnp.inf, F32)
    acc_sc[...] = jnp.zeros(acc_sc.shape, F32)

    def scores(j, hh, slot):
        row0 = pl.multiple_of(j * tk, tk)
        s_sc[slot] = jnp.dot(k_ref[pl.ds(row0, tk), hh * QK_PAD:(hh + 1) * QK_PAD],
                             q_ref[0, hh * QK_PAD:(hh + 1) * QK_PAD, :],
                             preferred_element_type=F32)

    def softmax_pv(j, hh, slot, masked):
        s = s_sc[slot]
        if masked:
            kc = lax.broadcasted_iota(jnp.int32, (tk, tq), 0) // CHUNK
            qc = lax.broadcasted_iota(jnp.int32, (tk, tq), 1) // CHUNK
            s = jnp.where(kc <= qc, s, -1e30)
        m_prev = m_sc[hh]
        m_new = jnp.maximum(m_prev, jnp.max(s, axis=0, keepdims=True))
        alpha = jnp.exp2(m_prev - m_new)
        p = jnp.exp2(s - m_new).astype(BF16)
        pv = jnp.dot(v_ref[j, hh * V_ROWS:(hh + 1) * V_ROWS, :], p,
                     preferred_element_type=F32)
        acc_sc[hh] = alpha * acc_sc[hh] + pv
        m_sc[hh] = m_new

    scores(0, 0, 0)

    def block(j, masked, next_j):
        for hh in range(MLA_HEADS):
            if hh + 1 < MLA_HEADS:
                scores(j, hh + 1, (hh + 1) % 2)
            elif next_j is not None:
                scores(next_j, 0, 0)
            softmax_pv(j, hh, hh % 2, masked)

    def run(j0, n):
        for t in range(n):
            block(j0 + t, False, j0 + t + 1)

    n_quads = lax.shift_right_logical(i, 2)

    def quad(q, carry):
        run(q * ATTN_UNROLL, ATTN_UNROLL)
        return carry

    lax.fori_loop(0, n_quads, quad, 0)
    done = n_quads * ATTN_UNROLL

    @pl.when((i & 2) != 0)
    def _():
        run(done, 2)

    @pl.when((i & 1) != 0)
    def _():
        run(done + (i & 2), 1)

    block(i, True, None)

    g = g_ref[...]
    for hh in range(MLA_HEADS):
        acc = acc_sc[hh]
        o_t = acc[0:V_HEAD] / acc[V_HEAD:V_HEAD + 1]
        o_t = o_t * lax.rsqrt(jnp.mean(o_t * o_t, axis=0, keepdims=True) + EPS)
        o_ref[:, hh * V_HEAD:(hh + 1) * V_HEAD] = (
            o_t.T * g[:, hh * V_HEAD:(hh + 1) * V_HEAD]).astype(o_ref.dtype)


def _attention(q_t, k, v_t, gn_b, bsz, seq):
    tq = ATTN_TQ
    tk = v_t.shape[2]
    assert tq == tk
    nq = seq // tq
    hq = MLA_HEADS * QK_PAD
    nblk = seq // tk
    return pl.pallas_call(
        _attn_kernel,
        grid=(bsz, nq),
        in_specs=[
            pl.BlockSpec((1, hq, tq), lambda b, i: (b * nq + i, 0, 0)),
            pl.BlockSpec(memory_space=pl.ANY),
            pl.BlockSpec(memory_space=pl.ANY),
            _resident(gn_b.shape),
        ],
        out_specs=pl.BlockSpec((tq, MLA_WIDTH), lambda b, i: (b * nq + i, 0)),
        out_shape=jax.ShapeDtypeStruct((bsz * seq, MLA_WIDTH), BF16),
        scratch_shapes=[
            pltpu.VMEM((seq, hq), BF16),
            pltpu.VMEM((nblk, MLA_HEADS * V_ROWS, tk), BF16),
            pltpu.SemaphoreType.DMA((2, nblk)),
            pltpu.VMEM((2, tk, tq), F32),
            pltpu.VMEM((MLA_HEADS, 1, tq), F32),
            pltpu.VMEM((MLA_HEADS, V_ROWS, tq), F32),
        ],
        compiler_params=_params(2),
        name="mla_attention",
    )(q_t, k, v_t, gn_b)


def _out_kernel(x_ref, ya_ref, yb_ref, mod_ref, wo_ref, ng_ref, w1_ref, w3_ref, w2_ref, fg_ref, o_ref):
    mod = mod_ref[0]
    for r in range(0, x_ref.shape[0], ROW_TILE):
        rows = slice(r, r + ROW_TILE)
        y = jnp.concatenate([ya_ref[rows, :], yb_ref[rows, :]], axis=1)
        x = x_ref[rows, :] + mod[5:6] * jnp.dot(y, wo_ref[...], preferred_element_type=F32)
        x = _ffn_update(x, mod[6:7], mod[7:8], mod[8:9], ng_ref[...], w1_ref, w3_ref, w2_ref)
        o_ref[rows, :] = x * _rms_scale(x) * fg_ref[...]


def _out_ffn2(x1, ya, yb, mod, w_out, norm_g, w1, w3, w2, final_g, tiles_per_batch):
    t, d = x1.shape
    tm = FFN_STEP_ROWS
    tiles_per_batch = tiles_per_batch * ROW_TILE // tm
    return pl.pallas_call(
        _out_kernel,
        grid=(t // tm,),
        in_specs=[
            pl.BlockSpec((tm, d), lambda i: (i, 0)),
            pl.BlockSpec((tm, CONV_WIDTH), lambda i: (i, 0)),
            pl.BlockSpec((tm, MLA_WIDTH), lambda i: (i, 0)),
            pl.BlockSpec((1, N_MOD, d), lambda i: (i // tiles_per_batch, 0, 0)),
            _resident(w_out.shape),
            _resident((1, d)),
            _resident(w1.shape), _resident(w3.shape), _resident(w2.shape),
            _resident((1, d)),
        ],
        out_specs=pl.BlockSpec((tm, d), lambda i: (i, 0)),
        out_shape=jax.ShapeDtypeStruct((t, d), F32),
        compiler_params=_params(),
        name="out_ffn2",
    )(x1, ya, yb, mod, w_out, norm_g, w1, w3, w2, final_g)


def _swap_halves(w, axis):
    lo, hi = jnp.split(w, 2, axis=axis)
    return jnp.concatenate([hi, lo], axis=axis)


def kernel(x, c, positions, ada_w, ada_b, norm_ffn1_g, ffn1_w1, ffn1_w3, ffn1_w2, norm_mix_g, w_in, conv_w, q_norm_g, w_uq, kv_norm_g, w_ukv, out_norm_g, w_out, norm_ffn2_g, ffn2_w1, ffn2_w3, ffn2_w2, final_norm_g):
    bsz, seq, d = x.shape
    depth = ada_w.shape[0]
    t = bsz * seq
    tiles_per_batch = seq // ROW_TILE
    half = QK_ROPE // 2

    assert depth == 1, "only the one-layer block is implemented"
    l = 0

    cs_t, cs_row, mod, (f1w1, f1w3, f1w2) = _prologue(
        positions, c, ada_w[l], ada_b[l], [ffn1_w1[l], ffn1_w3[l], ffn1_w2[l]])
    mod = mod.reshape(bsz, N_MOD, d)

    xf = x.reshape(t, d)

    wq = w_uq[l].T.reshape(MLA_HEADS, QK_NOPE + QK_ROPE, Q_LORA)
    wq_rope = wq[:, QK_NOPE:]
    wq_t = jnp.concatenate([wq, _swap_halves(wq_rope, 1)], axis=1).reshape(
        MLA_HEADS * QK_PAD, Q_LORA).astype(BF16)
    wkv = w_ukv[l].reshape(KV_LORA, MLA_HEADS, QK_NOPE + V_HEAD)
    wk = wkv[:, :, :QK_NOPE].reshape(KV_LORA, MLA_HEADS * QK_NOPE).astype(BF16)
    wv_t = wkv[:, :, QK_NOPE:].reshape(KV_LORA, MLA_WIDTH).T.astype(BF16)

    x1, (w_in_t, wo, f2w1, f2w3, f2w2) = _ffn1(
        xf, mod, norm_ffn1_g[l].reshape(1, d), f1w1, f1w3, f1w2, tiles_per_batch,
        [w_in[l].T, w_out[l], ffn2_w1[l], ffn2_w3[l], ffn2_w2[l]])
    ya, q_t, k, v_t = _proj(
        x1, mod, norm_mix_g[l].reshape(1, d), w_in_t, conv_w[l],
        out_norm_g[l, :CONV_WIDTH].reshape(1, CONV_WIDTH), q_norm_g[l].reshape(1, Q_LORA), wq_t,
        kv_norm_g[l].reshape(1, KV_LORA), wk, wv_t, cs_row, cs_t, tiles_per_batch)
    yb = _attention(q_t, k, v_t, out_norm_g[l, CONV_WIDTH:].reshape(1, MLA_WIDTH), bsz, seq)
    xf = _out_ffn2(x1, ya, yb, mod, wo, norm_ffn2_g[l].reshape(1, d), f2w1, f2w3, f2w2,
                   final_norm_g.reshape(1, d), tiles_per_batch)
    return xf.reshape(bsz, seq, d)
```

```python
import functools
import math

import jax
import jax.numpy as jnp
from jax import lax
from jax.experimental import pallas as pl
from jax.experimental.pallas import tpu as pltpu

CHUNK = 64
EPS = 1e-6
N_MOD = 9
CONV_WIDTH = 512
CONV_GROUPS = 8
CONV_K = 3
MLA_HEADS = 4
QK_NOPE = 128
QK_ROPE = 64
V_HEAD = 128
Q_LORA = 384
KV_LORA = 256
ROPE_THETA = 10000.0
MLA_WIDTH = MLA_HEADS * V_HEAD

LANES = 128
SUBLANES = 8
MXU_TILE = 256
QK_PAD = MXU_TILE
BF16_ROWS = 16
V_ROWS = V_HEAD + BF16_ROWS
VMEM_LIMIT_BYTES = 56 * 1024 * 1024

ROW_TILE = 512
FFN_STEP_ROWS = 1024
ATTN_TQ = 512
ATTN_UNROLL = 4
PROLOGUE_STEPS = 8

F32 = jnp.float32
BF16 = jnp.bfloat16


def _params(n_axes=1, flags=None):
    return pltpu.CompilerParams(
        dimension_semantics=("arbitrary",) * n_axes,
        vmem_limit_bytes=VMEM_LIMIT_BYTES,
        flags=flags,
    )


def _resident(shape):
    nd = len(shape)
    return pl.BlockSpec(shape, lambda *_: (0,) * nd, pipeline_mode=pl.Buffered(1))


def _mod_rows(mod_ref):
    row = mod_ref[0]
    d = row.shape[1] // N_MOD
    return [row[:, j * d:(j + 1) * d] for j in range(N_MOD)]


def _rms_scale(x):
    return lax.rsqrt(jnp.mean(x * x, axis=-1, keepdims=True) + EPS)


def _cast_streams(weights, n_steps):
    in_specs, out_specs, out_shapes = [], [], []
    for w in weights:
        rows, width = w.shape
        s = max(k for k in range(1, n_steps + 1)
                if rows % k == 0 and (rows // k) % BF16_ROWS == 0)
        spec = pl.BlockSpec((rows // s, width), lambda i, s=s: (jnp.minimum(i, s - 1), 0))
        in_specs.append(spec)
        out_specs.append(spec)
        out_shapes.append(jax.ShapeDtypeStruct((rows, width), BF16))
    return in_specs, out_specs, out_shapes


def _run_casts(src_refs, dst_refs):
    for src, dst in zip(src_refs, dst_refs):
        dst[...] = src[...].astype(dst.dtype)


def _prologue_kernel(pos_ref, inv_ref, cb_ref, w_ref, b_ref, *refs):
    n_cast = (len(refs) - 3) // 2
    cst_ref, csr_ref, mod_ref = refs[n_cast:n_cast + 3]

    ang = pos_ref[...].astype(F32) * inv_ref[...]
    c = jnp.cos(ang)
    s = jnp.sin(ang)
    cst = jnp.concatenate([c, c, -s, s], axis=0)
    cst_ref[...] = cst
    csr_ref[...] = cst.T

    w = w_ref[...]
    tn = w.shape[1]
    for b in range(cb_ref.shape[0]):
        cb = cb_ref[b]
        act = cb * jax.nn.sigmoid(cb)
        cols = [jnp.sum(w[:, j * LANES:(j + 1) * LANES] * act, axis=0, keepdims=True)
                for j in range(tn // LANES)]
        mod_ref[b] = jnp.concatenate(cols, axis=1) + b_ref[...]

    _run_casts(refs[:n_cast], refs[n_cast + 3:])


def _prologue(positions, c, ada_w, ada_b, cast_weights):
    half = QK_ROPE // 2
    t = positions.size
    n_steps = PROLOGUE_STEPS
    tc = t // n_steps
    bsz, d = c.shape
    n = ada_w.shape[1]
    tn = n // n_steps
    assert t % n_steps == 0 and n % n_steps == 0 and tn % LANES == 0 and tc % LANES == 0
    inv_freq = ROPE_THETA ** (-jnp.arange(0, QK_ROPE, 2, dtype=F32) / QK_ROPE)
    cb = jnp.broadcast_to(c[:, :, None], (bsz, d, LANES))
    c_in, c_out, c_shapes = _cast_streams(cast_weights, n_steps)
    outs = pl.pallas_call(
        _prologue_kernel,
        grid=(n_steps,),
        in_specs=[pl.BlockSpec((1, tc), lambda i: (0, i)),
                  pl.BlockSpec((half, 1), lambda i: (0, 0)),
                  pl.BlockSpec((bsz, d, LANES), lambda i: (0, 0, 0)),
                  pl.BlockSpec((d, tn), lambda i: (0, i)),
                  pl.BlockSpec((1, tn), lambda i: (0, i))] + c_in,
        out_specs=[pl.BlockSpec((4 * half, tc), lambda i: (0, i)),
                   pl.BlockSpec((tc, 4 * half), lambda i: (i, 0)),
                   pl.BlockSpec((bsz, 1, tn), lambda i: (0, 0, i))] + c_out,
        out_shape=[jax.ShapeDtypeStruct((4 * half, t), F32),
                   jax.ShapeDtypeStruct((t, 4 * half), F32),
                   jax.ShapeDtypeStruct((bsz, 1, n), F32)] + c_shapes,
        compiler_params=_params(),
        name="prologue",
    )(positions.reshape(1, t), inv_freq.reshape(half, 1), cb, ada_w, ada_b.reshape(1, n), *cast_weights)
    return outs[0], outs[1], outs[2], outs[3:]


def _ffn_update(x, shift, scale, gate, norm_g, w1_ref, w3_ref, w2_ref):
    h = (x * _rms_scale(x) * (norm_g * (1.0 + scale)) + shift).astype(BF16)
    a = jnp.dot(h, w1_ref[...], preferred_element_type=F32)
    b = jnp.dot(h, w3_ref[...], preferred_element_type=F32)
    g = (a * jax.nn.sigmoid(a) * b).astype(BF16)
    o = jnp.dot(g, w2_ref[...], preferred_element_type=F32)
    return x + (0.5 * gate) * o


def _mla_weight_prep(wuq_ref, wukv_ref, wq_ref, wk_ref, wv_ref):
    per_head = QK_NOPE + QK_ROPE
    half = QK_ROPE // 2
    wt = wuq_ref[...].T
    rows = []
    for hh in range(MLA_HEADS):
        base = hh * per_head
        rope0 = base + QK_NOPE
        rows += [wt[base:base + per_head], wt[rope0 + half:rope0 + QK_ROPE], wt[rope0:rope0 + half]]
    wq_ref[...] = jnp.concatenate(rows, axis=0).astype(wq_ref.dtype)
    wkv = wukv_ref[...]
    kv_w = QK_NOPE + V_HEAD
    wk_ref[...] = jnp.concatenate(
        [wkv[:, hh * kv_w:hh * kv_w + QK_NOPE] for hh in range(MLA_HEADS)], axis=1).astype(wk_ref.dtype)
    wv_ref[...] = jnp.concatenate(
        [wkv[:, hh * kv_w + QK_NOPE:(hh + 1) * kv_w] for hh in range(MLA_HEADS)],
        axis=1).T.astype(wv_ref.dtype)


def _ffn1_kernel(x_ref, mod_ref, ng_ref, w1_ref, w3_ref, w2_ref, wuq_ref, wukv_ref, *refs):
    n_cast = (len(refs) - 4) // 2
    o_ref, wq_ref, wk_ref, wv_ref = refs[n_cast:n_cast + 4]
    mod = _mod_rows(mod_ref)
    for r in range(0, x_ref.shape[0], ROW_TILE):
        rows = slice(r, r + ROW_TILE)
        o_ref[rows, :] = _ffn_update(x_ref[rows, :], mod[0], mod[1], mod[2], ng_ref[...],
                                     w1_ref, w3_ref, w2_ref)
    _run_casts(refs[:n_cast], refs[n_cast + 4:])

    @pl.when(pl.program_id(0) == 0)
    def _():
        _mla_weight_prep(wuq_ref, wukv_ref, wq_ref, wk_ref, wv_ref)


def _ffn1(x2d, mod, norm_g, w1, w3, w2, tiles_per_batch, w_uq, w_ukv, cast_weights):
    t, d = x2d.shape
    tm = FFN_STEP_ROWS
    tiles_per_batch = tiles_per_batch * ROW_TILE // tm
    n_steps = t // tm
    c_in, c_out, c_shapes = _cast_streams(cast_weights, n_steps)
    prep_shapes = [(MLA_HEADS * QK_PAD, Q_LORA), (KV_LORA, MLA_HEADS * QK_NOPE), (MLA_WIDTH, KV_LORA)]
    outs = pl.pallas_call(
        _ffn1_kernel,
        grid=(n_steps,),
        in_specs=[
            pl.BlockSpec((tm, d), lambda i: (i, 0)),
            pl.BlockSpec((1, 1, N_MOD * d), lambda i: (i // tiles_per_batch, 0, 0)),
            _resident((1, d)),
            _resident(w1.shape), _resident(w3.shape), _resident(w2.shape),
            _resident(w_uq.shape), _resident(w_ukv.shape),
        ] + c_in,
        out_specs=[pl.BlockSpec((tm, d), lambda i: (i, 0))]
        + [pl.BlockSpec(s, lambda i: (0, 0)) for s in prep_shapes] + c_out,
        out_shape=[jax.ShapeDtypeStruct((t, d), F32)]
        + [jax.ShapeDtypeStruct(s, BF16) for s in prep_shapes] + c_shapes,
        compiler_params=_params(),
        name="ffn1",
    )(x2d, mod, norm_g, w1, w3, w2, w_uq, w_ukv, *cast_weights)
    return outs[0], outs[1:4], outs[4:]


def _proj_kernel(x_ref, mod_ref, ng_ref, wint_ref, cw_ref, gna_ref, qng_ref, wq_ref, kvng_ref,
                 wk_ref, wv_ref, cs_ref, cst_ref,
                 ya_ref, q_ref, k_ref, v_ref, zbuf_a, zbuf_b, ubuf, *, tiles_per_batch):
    i = pl.program_id(0)

    @pl.when(i == 0)
    def _():
        zbuf_b[...] = jnp.zeros(zbuf_b.shape, F32)
        ubuf[0:SUBLANES, :] = jnp.zeros((SUBLANES, CONV_WIDTH), F32)

    def step(z_new, z_old):
        x = x_ref[...]
        mod = _mod_rows(mod_ref)
        h = (x * _rms_scale(x) * (ng_ref[...] * (1.0 + mod[4])) + mod[3]).astype(BF16)
        post = _mixer_post_pieces(z_old, i - 1, cw_ref, gna_ref, qng_ref, wq_ref, kvng_ref, wk_ref,
                                  wv_ref, cs_ref, cst_ref, ya_ref, q_ref, k_ref, v_ref, ubuf,
                                  tiles_per_batch)
        n_out = wint_ref.shape[0]
        n_main = n_out // MXU_TILE
        half = QK_ROPE // 2
        assert n_out - n_main * MXU_TILE + QK_ROPE == MXU_TILE and len(post) == n_main + 1
        nt = (((1,), (1,)), ((), ()))
        for c in range(n_main + 1):
            cols = slice(c * MXU_TILE, (c + 1) * MXU_TILE)
            if c < n_main:
                w_c = wint_ref[cols, :]
            else:
                w_c = jnp.concatenate([wint_ref[n_main * MXU_TILE:, :], wint_ref[n_out - half:, :],
                                       wint_ref[n_out - QK_ROPE:n_out - half, :]], axis=0)
            z_new[:, cols] = lax.dot_general(h, w_c, nt, preferred_element_type=F32)
            post[c]()

    pl.when(i % 2 == 0)(functools.partial(step, zbuf_a, zbuf_b))
    pl.when(i % 2 == 1)(functools.partial(step, zbuf_b, zbuf_a))


def _mixer_post_pieces(z, tile, cw_ref, gna_ref, qng_ref, wq_ref, kvng_ref, wk_ref, wv_ref, cs_ref,
                       cst_ref, ya_ref, q_ref, k_ref, v_ref, ubuf, tiles_per_batch):
    tm = z.shape[0]
    cwid = CONV_WIDTH
    c0 = 3 * cwid
    nt = (((1,), (1,)), ((), ()))
    scale = (QK_NOPE + QK_ROPE) ** -0.5 * math.log2(math.e)
    state = {}

    def q_head(hh):
        if hh == 0:
            cq = z[:, c0:c0 + Q_LORA]
            state["cqn"] = (cq * _rms_scale(cq) * qng_ref[...]).astype(BF16)
        base = hh * QK_PAD
        qt = lax.dot_general(wq_ref[base:base + QK_PAD, :], state["cqn"], nt,
                             preferred_element_type=F32)
        ct = cst_ref[0:QK_ROPE, :]
        st = cst_ref[QK_ROPE:2 * QK_ROPE, :]
        rope = qt[QK_NOPE:QK_NOPE + QK_ROPE] * ct + qt[QK_NOPE + QK_ROPE:QK_PAD] * st
        q_ref[0, base:base + QK_PAD, :] = jnp.concatenate(
            [qt[0:QK_NOPE] * scale, rope * scale, jnp.zeros((QK_ROPE, tm), F32)],
            axis=0).astype(q_ref.dtype)

    def kv():
        ckv = z[:, c0 + Q_LORA:c0 + Q_LORA + KV_LORA]
        ckvn = (ckv * _rms_scale(ckv) * kvng_ref[...]).astype(BF16)
        kn = jnp.dot(ckvn, wk_ref[...], preferred_element_type=F32)
        krr = z[:, c0 + Q_LORA + KV_LORA:c0 + Q_LORA + KV_LORA + LANES]
        a = krr * cs_ref[...]
        kr = a + pltpu.roll(a, QK_ROPE, axis=1)
        kp = []
        for hh in range(MLA_HEADS):
            kp += [kn[:, hh * QK_NOPE:(hh + 1) * QK_NOPE], kr]
        k_ref[...] = jnp.concatenate(kp, axis=1).astype(k_ref.dtype)
        vt = lax.dot_general(wv_ref[...], ckvn, nt, preferred_element_type=F32)
        ones = jnp.ones((V_ROWS - V_HEAD, tm), F32)
        vp = []
        for hh in range(MLA_HEADS):
            vp += [vt[hh * V_HEAD:(hh + 1) * V_HEAD], ones]
        v_ref[0] = jnp.concatenate(vp, axis=0).astype(v_ref.dtype)

    def conv_tile(j):
        cols = slice(j * LANES, (j + 1) * LANES)
        ubuf[0:SUBLANES, cols] = jnp.where(tile % tiles_per_batch == 0, 0.0, ubuf[0:SUBLANES, cols])
        u = z[:, cwid + j * LANES:cwid + (j + 1) * LANES] * z[:, 2 * cwid + j * LANES:2 * cwid + (j + 1) * LANES]
        ubuf[SUBLANES:SUBLANES + tm, cols] = u
        u1 = ubuf[SUBLANES - 1:SUBLANES - 1 + tm, cols]
        u2 = ubuf[SUBLANES - 2:SUBLANES - 2 + tm, cols]
        cw = cw_ref[:, cols]
        ya = z[:, cols] * (cw[0:1] * u2 + cw[1:2] * u1 + cw[2:3] * u)
        ubuf[0:SUBLANES, cols] = ubuf[tm:tm + SUBLANES, cols]
        gw = cwid // CONV_GROUPS
        low = lax.broadcasted_iota(jnp.int32, (tm, LANES), 1) < gw
        sq = ya * ya
        s_lo = jnp.sum(jnp.where(low, sq, 0.0), axis=-1, keepdims=True)
        s_hi = jnp.sum(jnp.where(low, 0.0, sq), axis=-1, keepdims=True)
        ms = jnp.where(low, s_lo, s_hi) * (1.0 / gw)
        ya_ref[:, cols] = (ya * lax.rsqrt(ms + EPS) * gna_ref[:, cols]).astype(ya_ref.dtype)

    pieces = [functools.partial(q_head, hh) for hh in range(MLA_HEADS)] + [kv]
    pieces += [functools.partial(conv_tile, j) for j in range(cwid // LANES)]
    return pieces


def _proj(x1, mod, norm_g, w_in_t, conv_w, gn_a, q_norm_g, wq_t, kv_norm_g, wk, wv_t, cs_row, cs_t,
          tiles_per_batch):
    t, d = x1.shape
    tm = ROW_TILE
    nt = t // tm
    hq = MLA_HEADS * QK_PAD

    def cur(i):
        return jnp.minimum(i, nt - 1)

    def prev(i):
        return jnp.maximum(i - 1, 0)

    return pl.pallas_call(
        functools.partial(_proj_kernel, tiles_per_batch=tiles_per_batch),
        grid=(nt + 1,),
        in_specs=[
            pl.BlockSpec((tm, d), lambda i: (cur(i), 0)),
            pl.BlockSpec((1, 1, N_MOD * d), lambda i: (cur(i) // tiles_per_batch, 0, 0)),
            _resident((1, d)),
            _resident(w_in_t.shape),
            _resident(conv_w.shape),
            _resident(gn_a.shape),
            _resident(q_norm_g.shape),
            _resident(wq_t.shape),
            _resident(kv_norm_g.shape),
            _resident(wk.shape),
            _resident(wv_t.shape),
            pl.BlockSpec((tm, LANES), lambda i: (prev(i), 0)),
            pl.BlockSpec((LANES, tm), lambda i: (0, prev(i))),
        ],
        out_specs=[
            pl.BlockSpec((tm, CONV_WIDTH), lambda i: (prev(i), 0)),
            pl.BlockSpec((1, hq, tm), lambda i: (prev(i), 0, 0)),
            pl.BlockSpec((tm, hq), lambda i: (prev(i), 0)),
            pl.BlockSpec((1, MLA_HEADS * V_ROWS, tm), lambda i: (prev(i), 0, 0)),
        ],
        out_shape=[
            jax.ShapeDtypeStruct((t, CONV_WIDTH), BF16),
            jax.ShapeDtypeStruct((nt, hq, tm), BF16),
            jax.ShapeDtypeStruct((t, hq), BF16),
            jax.ShapeDtypeStruct((nt, MLA_HEADS * V_ROWS, tm), BF16),
        ],
        scratch_shapes=[pltpu.VMEM((tm, w_in_t.shape[0] + QK_ROPE), F32),
                        pltpu.VMEM((tm, w_in_t.shape[0] + QK_ROPE), F32),
                        pltpu.VMEM((tm + SUBLANES, CONV_WIDTH), F32)],
        compiler_params=_params(),
        name="mixer_proj",
    )(x1, mod, norm_g, w_in_t, conv_w, gn_a, q_norm_g, wq_t, kv_norm_g, wk, wv_t, cs_row, cs_t)


def _attn_kernel(q_ref, k_hbm, v_hbm, g_ref, o_ref, k_ref, v_ref, kv_sem, s_sc, m_sc, acc_sc):
    b = pl.program_id(0)
    i = pl.program_id(1)
    tq = q_ref.shape[2]
    nblk, _, tk = v_ref.shape

    def k_copy(j):
        return pltpu.make_async_copy(k_hbm.at[pl.ds((b * nblk + j) * tk, tk), :],
                                     k_ref.at[pl.ds(j * tk, tk), :], kv_sem.at[0, j])

    def v_copy(j):
        return pltpu.make_async_copy(v_hbm.at[b * nblk + j], v_ref.at[j], kv_sem.at[1, j])

    @pl.when(i == 0)
    def _():
        for j in range(nblk):
            k_copy(j).start()
            v_copy(j).start()

    k_copy(i).wait()
    v_copy(i).wait()

    m_sc[...] = jnp.full(m_sc.shape, -jnp.inf, F32)
    acc_sc[...] = jnp.zeros(acc_sc.shape, F32)

    def scores(j, hh, slot):
        row0 = pl.multiple_of(j * tk, tk)
        s_sc[slot] = jnp.dot(k_ref[pl.ds(row0, tk), hh * QK_PAD:(hh + 1) * QK_PAD],
                             q_ref[0, hh * QK_PAD:(hh + 1) * QK_PAD, :],
                             preferred_element_type=F32)

    def softmax_pv(j, hh, slot, masked):
        s = s_sc[slot]
        if masked:
            kc = lax.broadcasted_iota(jnp.int32, (tk, tq), 0) // CHUNK
            qc = lax.broadcasted_iota(jnp.int32, (tk, tq), 1) // CHUNK
            s = jnp.where(kc <= qc, s, -1e30)
        m_prev = m_sc[hh]
        m_new = jnp.maximum(m_prev, jnp.max(s, axis=0, keepdims=True))
        alpha = jnp.exp2(m_prev - m_new)
        p = jnp.exp2(s - m_new).astype(BF16)
        pv = jnp.dot(v_ref[j, hh * V_ROWS:(hh + 1) * V_ROWS, :], p,
                     preferred_element_type=F32)
        acc_sc[hh] = alpha * acc_sc[hh] + pv
        m_sc[hh] = m_new

    scores(0, 0, 0)

    def block(j, masked, next_j):
        for hh in range(MLA_HEADS):
            if hh + 1 < MLA_HEADS:
                scores(j, hh + 1, (hh + 1) % 2)
            elif next_j is not None:
                scores(next_j, 0, 0)
            softmax_pv(j, hh, hh % 2, masked)

    def run(j0, n):
        for t in range(n):
            block(j0 + t, False, j0 + t + 1)

    n_quads = lax.shift_right_logical(i, 2)

    def quad(q, carry):
        run(q * ATTN_UNROLL, ATTN_UNROLL)
        return carry

    lax.fori_loop(0, n_quads, quad, 0)
    done = n_quads * ATTN_UNROLL

    @pl.when((i & 2) != 0)
    def _():
        run(done, 2)

    @pl.when((i & 1) != 0)
    def _():
        run(done + (i & 2), 1)

    block(i, True, None)

    g = g_ref[...]
    for hh in range(MLA_HEADS):
        acc = acc_sc[hh]
        o_t = acc[0:V_HEAD] / acc[V_HEAD:V_HEAD + 1]
        o_t = o_t * lax.rsqrt(jnp.mean(o_t * o_t, axis=0, keepdims=True) + EPS)
        o_ref[:, hh * V_HEAD:(hh + 1) * V_HEAD] = (
            o_t.T * g[:, hh * V_HEAD:(hh + 1) * V_HEAD]).astype(o_ref.dtype)


def _attention(q_t, k, v_t, gn_b, bsz, seq):
    tq = ATTN_TQ
    tk = v_t.shape[2]
    assert tq == tk
    nq = seq // tq
    hq = MLA_HEADS * QK_PAD
    nblk = seq // tk
    return pl.pallas_call(
        _attn_kernel,
        grid=(bsz, nq),
        in_specs=[
            pl.BlockSpec((1, hq, tq), lambda b, i: (b * nq + i, 0, 0)),
            pl.BlockSpec(memory_space=pl.ANY),
            pl.BlockSpec(memory_space=pl.ANY),
            _resident(gn_b.shape),
        ],
        out_specs=pl.BlockSpec((tq, MLA_WIDTH), lambda b, i: (b * nq + i, 0)),
        out_shape=jax.ShapeDtypeStruct((bsz * seq, MLA_WIDTH), BF16),
        scratch_shapes=[
            pltpu.VMEM((seq, hq), BF16),
            pltpu.VMEM((nblk, MLA_HEADS * V_ROWS, tk), BF16),
            pltpu.SemaphoreType.DMA((2, nblk)),
            pltpu.VMEM((2, tk, tq), F32),
            pltpu.VMEM((MLA_HEADS, 1, tq), F32),
            pltpu.VMEM((MLA_HEADS, V_ROWS, tq), F32),
        ],
        compiler_params=_params(2),
        name="mla_attention",
    )(q_t, k, v_t, gn_b)


def _out_kernel(x_ref, ya_ref, yb_ref, mod_ref, wo_ref, ng_ref, w1_ref, w3_ref, w2_ref, fg_ref, o_ref):
    mod = _mod_rows(mod_ref)
    for r in range(0, x_ref.shape[0], ROW_TILE):
        rows = slice(r, r + ROW_TILE)
        y = jnp.concatenate([ya_ref[rows, :], yb_ref[rows, :]], axis=1)
        x = x_ref[rows, :] + mod[5] * jnp.dot(y, wo_ref[...], preferred_element_type=F32)
        x = _ffn_update(x, mod[6], mod[7], mod[8], ng_ref[...], w1_ref, w3_ref, w2_ref)
        o_ref[rows, :] = x * _rms_scale(x) * fg_ref[...]


def _out_ffn2(x1, ya, yb, mod, w_out, norm_g, w1, w3, w2, final_g, tiles_per_batch):
    t, d = x1.shape
    tm = FFN_STEP_ROWS
    tiles_per_batch = tiles_per_batch * ROW_TILE // tm
    return pl.pallas_call(
        _out_kernel,
        grid=(t // tm,),
        in_specs=[
            pl.BlockSpec((tm, d), lambda i: (i, 0)),
            pl.BlockSpec((tm, CONV_WIDTH), lambda i: (i, 0)),
            pl.BlockSpec((tm, MLA_WIDTH), lambda i: (i, 0)),
            pl.BlockSpec((1, 1, N_MOD * d), lambda i: (i // tiles_per_batch, 0, 0)),
            _resident(w_out.shape),
            _resident((1, d)),
            _resident(w1.shape), _resident(w3.shape), _resident(w2.shape),
            _resident((1, d)),
        ],
        out_specs=pl.BlockSpec((tm, d), lambda i: (i, 0)),
        out_shape=jax.ShapeDtypeStruct((t, d), F32),
        compiler_params=_params(),
        name="out_ffn2",
    )(x1, ya, yb, mod, w_out, norm_g, w1, w3, w2, final_g)


def kernel(x, c, positions, ada_w, ada_b, norm_ffn1_g, ffn1_w1, ffn1_w3, ffn1_w2, norm_mix_g, w_in, conv_w, q_norm_g, w_uq, kv_norm_g, w_ukv, out_norm_g, w_out, norm_ffn2_g, ffn2_w1, ffn2_w3, ffn2_w2, final_norm_g):
    bsz, seq, d = x.shape
    depth = ada_w.shape[0]
    t = bsz * seq
    tiles_per_batch = seq // ROW_TILE
    half = QK_ROPE // 2

    assert depth == 1, "only the one-layer block is implemented"
    l = 0

    cs_t, cs_row, mod, (f1w1, f1w3, f1w2) = _prologue(
        positions, c, ada_w[l], ada_b[l], [ffn1_w1[l], ffn1_w3[l], ffn1_w2[l]])

    xf = x.reshape(t, d)

    x1, (wq_t, wk, wv_t), (w_in_t, wo, f2w1, f2w3, f2w2) = _ffn1(
        xf, mod, norm_ffn1_g[l].reshape(1, d), f1w1, f1w3, f1w2, tiles_per_batch,
        w_uq[l], w_ukv[l], [w_in[l].T, w_out[l], ffn2_w1[l], ffn2_w3[l], ffn2_w2[l]])
    ya, q_t, k, v_t = _proj(
        x1, mod, norm_mix_g[l].reshape(1, d), w_in_t, conv_w[l],
        out_norm_g[l, :CONV_WIDTH].reshape(1, CONV_WIDTH), q_norm_g[l].reshape(1, Q_LORA), wq_t,
        kv_norm_g[l].reshape(1, KV_LORA), wk, wv_t, cs_row, cs_t, tiles_per_batch)
    yb = _attention(q_t, k, v_t, out_norm_g[l, CONV_WIDTH:].reshape(1, MLA_WIDTH), bsz, seq)
    xf = _out_ffn2(x1, ya, yb, mod, wo, norm_ffn2_g[l].reshape(1, d), f2w1, f2w3, f2w2,
                   final_norm_g.reshape(1, d), tiles_per_batch)
    return xf.reshape(bsz, seq, d)
```

```python
import functools
import math

import jax
import jax.numpy as jnp
from jax import lax
from jax.experimental import pallas as pl
from jax.experimental.pallas import tpu as pltpu

CHUNK = 64
EPS = 1e-6
N_MOD = 9
CONV_WIDTH = 512
CONV_GROUPS = 8
CONV_K = 3
MLA_HEADS = 4
QK_NOPE = 128
QK_ROPE = 64
V_HEAD = 128
Q_LORA = 384
KV_LORA = 256
ROPE_THETA = 10000.0
MLA_WIDTH = MLA_HEADS * V_HEAD

LANES = 128
SUBLANES = 8
MXU_TILE = 256
QK_PAD = MXU_TILE
BF16_ROWS = 16
V_ROWS = V_HEAD + BF16_ROWS
VMEM_LIMIT_BYTES = 56 * 1024 * 1024

ROW_TILE = 512
FFN_STEP_ROWS = 1024
ATTN_TQ = 512
ATTN_UNROLL = 4
PROLOGUE_STEPS = 8

F32 = jnp.float32
BF16 = jnp.bfloat16


def _params(n_axes=1, flags=None):
    return pltpu.CompilerParams(
        dimension_semantics=("arbitrary",) * n_axes,
        vmem_limit_bytes=VMEM_LIMIT_BYTES,
        flags=flags,
    )


def _resident(shape):
    nd = len(shape)
    return pl.BlockSpec(shape, lambda *_: (0,) * nd, pipeline_mode=pl.Buffered(1))


def _mod_rows(mod_ref):
    row = mod_ref[0]
    d = row.shape[1] // N_MOD
    return [row[:, j * d:(j + 1) * d] for j in range(N_MOD)]


def _rms_scale(x):
    return lax.rsqrt(jnp.mean(x * x, axis=-1, keepdims=True) + EPS)


def _cast_streams(weights, n_steps):
    in_specs, out_specs, out_shapes = [], [], []
    for w in weights:
        rows, width = w.shape
        s = max(k for k in range(1, n_steps + 1)
                if rows % k == 0 and (rows // k) % BF16_ROWS == 0)
        spec = pl.BlockSpec((rows // s, width), lambda i, s=s: (jnp.minimum(i, s - 1), 0))
        in_specs.append(spec)
        out_specs.append(spec)
        out_shapes.append(jax.ShapeDtypeStruct((rows, width), BF16))
    return in_specs, out_specs, out_shapes


def _run_casts(src_refs, dst_refs):
    for src, dst in zip(src_refs, dst_refs):
        dst[...] = src[...].astype(dst.dtype)


def _prologue_kernel(pos_ref, inv_ref, cb_ref, w_ref, b_ref, *refs):
    n_cast = (len(refs) - 3) // 2
    cst_ref, csr_ref, mod_ref = refs[n_cast:n_cast + 3]

    ang = pos_ref[...].astype(F32) * inv_ref[...]
    c = jnp.cos(ang)
    s = jnp.sin(ang)
    cst = jnp.concatenate([c, c, -s, s], axis=0)
    cst_ref[...] = cst
    csr_ref[...] = cst.T

    w = w_ref[...]
    tn = w.shape[1]
    for b in range(cb_ref.shape[0]):
        cb = cb_ref[b]
        act = cb * jax.nn.sigmoid(cb)
        cols = [jnp.sum(w[:, j * LANES:(j + 1) * LANES] * act, axis=0, keepdims=True)
                for j in range(tn // LANES)]
        mod_ref[b] = jnp.concatenate(cols, axis=1) + b_ref[...]

    _run_casts(refs[:n_cast], refs[n_cast + 3:])


def _prologue(positions, c, ada_w, ada_b, cast_weights):
    half = QK_ROPE // 2
    t = positions.size
    n_steps = PROLOGUE_STEPS
    tc = t // n_steps
    bsz, d = c.shape
    n = ada_w.shape[1]
    tn = n // n_steps
    assert t % n_steps == 0 and n % n_steps == 0 and tn % LANES == 0 and tc % LANES == 0
    inv_freq = ROPE_THETA ** (-jnp.arange(0, QK_ROPE, 2, dtype=F32) / QK_ROPE)
    cb = jnp.broadcast_to(c[:, :, None], (bsz, d, LANES))
    c_in, c_out, c_shapes = _cast_streams(cast_weights, n_steps)
    outs = pl.pallas_call(
        _prologue_kernel,
        grid=(n_steps,),
        in_specs=[pl.BlockSpec((1, tc), lambda i: (0, i)),
                  pl.BlockSpec((half, 1), lambda i: (0, 0)),
                  pl.BlockSpec((bsz, d, LANES), lambda i: (0, 0, 0)),
                  pl.BlockSpec((d, tn), lambda i: (0, i)),
                  pl.BlockSpec((1, tn), lambda i: (0, i))] + c_in,
        out_specs=[pl.BlockSpec((4 * half, tc), lambda i: (0, i)),
                   pl.BlockSpec((tc, 4 * half), lambda i: (i, 0)),
                   pl.BlockSpec((bsz, 1, tn), lambda i: (0, 0, i))] + c_out,
        out_shape=[jax.ShapeDtypeStruct((4 * half, t), F32),
                   jax.ShapeDtypeStruct((t, 4 * half), F32),
                   jax.ShapeDtypeStruct((bsz, 1, n), F32)] + c_shapes,
        compiler_params=_params(),
        name="prologue",
    )(positions.reshape(1, t), inv_freq.reshape(half, 1), cb, ada_w, ada_b.reshape(1, n), *cast_weights)
    return outs[0], outs[1], outs[2], outs[3:]


def _ffn_update(x, shift, scale, gate, norm_g, w1_ref, w3_ref, w2_ref):
    h = (x * _rms_scale(x) * (norm_g * (1.0 + scale)) + shift).astype(BF16)
    a = jnp.dot(h, w1_ref[...], preferred_element_type=F32)
    b = jnp.dot(h, w3_ref[...], preferred_element_type=F32)
    g = (a * jax.nn.sigmoid(a) * b).astype(BF16)
    o = jnp.dot(g, w2_ref[...], preferred_element_type=F32)
    return x + (0.5 * gate) * o


def _mla_weight_prep(wuq_ref, wukv_ref, wq_ref, wk_ref, wv_ref):
    per_head = QK_NOPE + QK_ROPE
    half = QK_ROPE // 2
    wt = wuq_ref[...].T
    rows = []
    for hh in range(MLA_HEADS):
        base = hh * per_head
        rope0 = base + QK_NOPE
        rows += [wt[base:base + per_head], wt[rope0 + half:rope0 + QK_ROPE], wt[rope0:rope0 + half]]
    wq_ref[...] = jnp.concatenate(rows, axis=0).astype(wq_ref.dtype)
    wkv = wukv_ref[...]
    kv_w = QK_NOPE + V_HEAD
    wk_ref[...] = jnp.concatenate(
        [wkv[:, hh * kv_w:hh * kv_w + QK_NOPE] for hh in range(MLA_HEADS)], axis=1).astype(wk_ref.dtype)
    wv_ref[...] = jnp.concatenate(
        [wkv[:, hh * kv_w + QK_NOPE:(hh + 1) * kv_w] for hh in range(MLA_HEADS)],
        axis=1).T.astype(wv_ref.dtype)


def _ffn1_kernel(x_ref, mod_ref, ng_ref, w1_ref, w3_ref, w2_ref, wuq_ref, wukv_ref, *refs):
    n_cast = (len(refs) - 4) // 2
    o_ref, wq_ref, wk_ref, wv_ref = refs[n_cast:n_cast + 4]
    mod = _mod_rows(mod_ref)
    for r in range(0, x_ref.shape[0], ROW_TILE):
        rows = slice(r, r + ROW_TILE)
        o_ref[rows, :] = _ffn_update(x_ref[rows, :], mod[0], mod[1], mod[2], ng_ref[...],
                                     w1_ref, w3_ref, w2_ref)
    _run_casts(refs[:n_cast], refs[n_cast + 4:])

    @pl.when(pl.program_id(0) == 0)
    def _():
        _mla_weight_prep(wuq_ref, wukv_ref, wq_ref, wk_ref, wv_ref)


def _ffn1(x2d, mod, norm_g, w1, w3, w2, tiles_per_batch, w_uq, w_ukv, cast_weights):
    t, d = x2d.shape
    tm = FFN_STEP_ROWS
    tiles_per_batch = tiles_per_batch * ROW_TILE // tm
    n_steps = t // tm
    c_in, c_out, c_shapes = _cast_streams(cast_weights, n_steps)
    prep_shapes = [(MLA_HEADS * QK_PAD, Q_LORA), (KV_LORA, MLA_HEADS * QK_NOPE), (MLA_WIDTH, KV_LORA)]
    outs = pl.pallas_call(
        _ffn1_kernel,
        grid=(n_steps,),
        in_specs=[
            pl.BlockSpec((tm, d), lambda i: (i, 0)),
            pl.BlockSpec((1, 1, N_MOD * d), lambda i: (i // tiles_per_batch, 0, 0)),
            _resident((1, d)),
            _resident(w1.shape), _resident(w3.shape), _resident(w2.shape),
            _resident(w_uq.shape), _resident(w_ukv.shape),
        ] + c_in,
        out_specs=[pl.BlockSpec((tm, d), lambda i: (i, 0))]
        + [pl.BlockSpec(s, lambda i: (0, 0)) for s in prep_shapes] + c_out,
        out_shape=[jax.ShapeDtypeStruct((t, d), F32)]
        + [jax.ShapeDtypeStruct(s, BF16) for s in prep_shapes] + c_shapes,
        compiler_params=_params(),
        name="ffn1",
    )(x2d, mod, norm_g, w1, w3, w2, w_uq, w_ukv, *cast_weights)
    return outs[0], outs[1:4], outs[4:]


def _proj_kernel(x_ref, mod_ref, ng_ref, wint_ref, cw_ref, gna_ref, qng_ref, wq_ref, kvng_ref,
                 wk_ref, wv_ref, cs_ref, cst_ref,
                 ya_ref, q_ref, k_ref, v_ref, zbuf_a, zbuf_b, ubuf, *, tiles_per_batch):
    i = pl.program_id(0)

    @pl.when(i == 0)
    def _():
        zbuf_b[...] = jnp.zeros(zbuf_b.shape, F32)
        ubuf[0:SUBLANES, :] = jnp.zeros((SUBLANES, CONV_WIDTH), F32)

    def step(z_new, z_old):
        x = x_ref[...]
        mod = _mod_rows(mod_ref)
        h = (x * _rms_scale(x) * (ng_ref[...] * (1.0 + mod[4])) + mod[3]).astype(BF16)
        post = _mixer_post_pieces(z_old, i - 1, cw_ref, gna_ref, qng_ref, wq_ref, kvng_ref, wk_ref,
                                  wv_ref, cs_ref, cst_ref, ya_ref, q_ref, k_ref, v_ref, ubuf,
                                  tiles_per_batch)
        n_out = wint_ref.shape[0]
        n_main = n_out // MXU_TILE
        half = QK_ROPE // 2
        assert n_out - n_main * MXU_TILE + QK_ROPE == MXU_TILE and len(post) == n_main + 1
        nt = (((1,), (1,)), ((), ()))
        for c in range(n_main + 1):
            cols = slice(c * MXU_TILE, (c + 1) * MXU_TILE)
            if c < n_main:
                w_c = wint_ref[cols, :]
            else:
                w_c = jnp.concatenate([wint_ref[n_main * MXU_TILE:, :], wint_ref[n_out - half:, :],
                                       wint_ref[n_out - QK_ROPE:n_out - half, :]], axis=0)
            z_new[:, cols] = lax.dot_general(h, w_c, nt, preferred_element_type=F32)
            post[c]()

    pl.when(i % 2 == 0)(functools.partial(step, zbuf_a, zbuf_b))
    pl.when(i % 2 == 1)(functools.partial(step, zbuf_b, zbuf_a))


def _mixer_post_pieces(z, tile, cw_ref, gna_ref, qng_ref, wq_ref, kvng_ref, wk_ref, wv_ref, cs_ref,
                       cst_ref, ya_ref, q_ref, k_ref, v_ref, ubuf, tiles_per_batch):
    tm = z.shape[0]
    cwid = CONV_WIDTH
    c0 = 3 * cwid
    nt = (((1,), (1,)), ((), ()))
    scale = (QK_NOPE + QK_ROPE) ** -0.5 * math.log2(math.e)
    state = {}

    def q_head(hh):
        if hh == 0:
            cq = z[:, c0:c0 + Q_LORA]
            state["cqn"] = (cq * _rms_scale(cq) * qng_ref[...]).astype(BF16)
        base = hh * QK_PAD
        qt = lax.dot_general(wq_ref[base:base + QK_PAD, :], state["cqn"], nt,
                             preferred_element_type=F32)
        ct = cst_ref[0:QK_ROPE, :]
        st = cst_ref[QK_ROPE:2 * QK_ROPE, :]
        rope = qt[QK_NOPE:QK_NOPE + QK_ROPE] * ct + qt[QK_NOPE + QK_ROPE:QK_PAD] * st
        q_ref[0, base:base + QK_PAD, :] = jnp.concatenate(
            [qt[0:QK_NOPE] * scale, rope * scale, jnp.zeros((QK_ROPE, tm), F32)],
            axis=0).astype(q_ref.dtype)

    def kv():
        ckv = z[:, c0 + Q_LORA:c0 + Q_LORA + KV_LORA]
        ckvn = (ckv * _rms_scale(ckv) * kvng_ref[...]).astype(BF16)
        kn = jnp.dot(ckvn, wk_ref[...], preferred_element_type=F32)
        krr = z[:, c0 + Q_LORA + KV_LORA:c0 + Q_LORA + KV_LORA + LANES]
        a = krr * cs_ref[...]
        kr = a + pltpu.roll(a, QK_ROPE, axis=1)
        kp = []
        for hh in range(MLA_HEADS):
            kp += [kn[:, hh * QK_NOPE:(hh + 1) * QK_NOPE], kr]
        k_ref[...] = jnp.concatenate(kp, axis=1).astype(k_ref.dtype)
        vt = lax.dot_general(wv_ref[...], ckvn, nt, preferred_element_type=F32)
        ones = jnp.ones((V_ROWS - V_HEAD, tm), F32)
        vp = []
        for hh in range(MLA_HEADS):
            vp += [vt[hh * V_HEAD:(hh + 1) * V_HEAD], ones]
        v_ref[0] = jnp.concatenate(vp, axis=0).astype(v_ref.dtype)

    def conv_tile(j):
        cols = slice(j * LANES, (j + 1) * LANES)
        ubuf[0:SUBLANES, cols] = jnp.where(tile % tiles_per_batch == 0, 0.0, ubuf[0:SUBLANES, cols])
        u = z[:, cwid + j * LANES:cwid + (j + 1) * LANES] * z[:, 2 * cwid + j * LANES:2 * cwid + (j + 1) * LANES]
        ubuf[SUBLANES:SUBLANES + tm, cols] = u
        u1 = ubuf[SUBLANES - 1:SUBLANES - 1 + tm, cols]
        u2 = ubuf[SUBLANES - 2:SUBLANES - 2 + tm, cols]
        cw = cw_ref[:, cols]
        ya = z[:, cols] * (cw[0:1] * u2 + cw[1:2] * u1 + cw[2:3] * u)
        ubuf[0:SUBLANES, cols] = ubuf[tm:tm + SUBLANES, cols]
        gw = cwid // CONV_GROUPS
        low = lax.broadcasted_iota(jnp.int32, (tm, LANES), 1) < gw
        sq = ya * ya
        s_lo = jnp.sum(jnp.where(low, sq, 0.0), axis=-1, keepdims=True)
        s_hi = jnp.sum(jnp.where(low, 0.0, sq), axis=-1, keepdims=True)
        ms = jnp.where(low, s_lo, s_hi) * (1.0 / gw)
        ya_ref[:, cols] = (ya * lax.rsqrt(ms + EPS) * gna_ref[:, cols]).astype(ya_ref.dtype)

    pieces = [functools.partial(q_head, hh) for hh in range(MLA_HEADS)] + [kv]
    pieces += [functools.partial(conv_tile, j) for j in range(cwid // LANES)]
    return pieces


def _proj(x1, mod, norm_g, w_in_t, conv_w, gn_a, q_norm_g, wq_t, kv_norm_g, wk, wv_t, cs_row, cs_t,
          tiles_per_batch):
    t, d = x1.shape
    tm = ROW_TILE
    nt = t // tm
    hq = MLA_HEADS * QK_PAD

    def cur(i):
        return jnp.minimum(i, nt - 1)

    def prev(i):
        return jnp.maximum(i - 1, 0)

    return pl.pallas_call(
        functools.partial(_proj_kernel, tiles_per_batch=tiles_per_batch),
        grid=(nt + 1,),
        in_specs=[
            pl.BlockSpec((tm, d), lambda i: (cur(i), 0)),
            pl.BlockSpec((1, 1, N_MOD * d), lambda i: (cur(i) // tiles_per_batch, 0, 0)),
            _resident((1, d)),
            _resident(w_in_t.shape),
            _resident(conv_w.shape),
            _resident(gn_a.shape),
            _resident(q_norm_g.shape),
            _resident(wq_t.shape),
            _resident(kv_norm_g.shape),
            _resident(wk.shape),
            _resident(wv_t.shape),
            pl.BlockSpec((tm, LANES), lambda i: (prev(i), 0)),
            pl.BlockSpec((LANES, tm), lambda i: (0, prev(i))),
        ],
        out_specs=[
            pl.BlockSpec((tm, CONV_WIDTH), lambda i: (prev(i), 0)),
            pl.BlockSpec((1, hq, tm), lambda i: (prev(i), 0, 0)),
            pl.BlockSpec((tm, hq), lambda i: (prev(i), 0)),
            pl.BlockSpec((1, MLA_HEADS * V_ROWS, tm), lambda i: (prev(i), 0, 0)),
        ],
        out_shape=[
            jax.ShapeDtypeStruct((t, CONV_WIDTH), BF16),
            jax.ShapeDtypeStruct((nt, hq, tm), BF16),
            jax.ShapeDtypeStruct((t, hq), BF16),
            jax.ShapeDtypeStruct((nt, MLA_HEADS * V_ROWS, tm), BF16),
        ],
        scratch_shapes=[pltpu.VMEM((tm, w_in_t.shape[0] + QK_ROPE), F32),
                        pltpu.VMEM((tm, w_in_t.shape[0] + QK_ROPE), F32),
                        pltpu.VMEM((tm + SUBLANES, CONV_WIDTH), F32)],
        compiler_params=_params(),
        name="mixer_proj",
    )(x1, mod, norm_g, w_in_t, conv_w, gn_a, q_norm_g, wq_t, kv_norm_g, wk, wv_t, cs_row, cs_t)


def _attn_kernel(q_ref, qn_ref, k_hbm, v_hbm, g_ref, o_ref, k_ref, v_ref, kv_sem, s_sc, m_sc, acc_sc):
    b = pl.program_id(0)
    i = pl.program_id(1)
    tq = q_ref.shape[2]
    nblk, _, tk = v_ref.shape

    def k_copy(j):
        return pltpu.make_async_copy(k_hbm.at[pl.ds((b * nblk + j) * tk, tk), :],
                                     k_ref.at[pl.ds(j * tk, tk), :], kv_sem.at[0, j])

    def v_copy(j):
        return pltpu.make_async_copy(v_hbm.at[b * nblk + j], v_ref.at[j], kv_sem.at[1, j])

    @pl.when(i == 0)
    def _():
        for j in range(nblk):
            k_copy(j).start()
            v_copy(j).start()

    k_copy(i).wait()
    v_copy(i).wait()

    m_sc[...] = jnp.full(m_sc.shape, -jnp.inf, F32)
    acc_sc[...] = jnp.zeros(acc_sc.shape, F32)

    def scores(j, hh, slot, q=q_ref):
        row0 = pl.multiple_of(j * tk, tk)
        s_sc[slot] = jnp.dot(k_ref[pl.ds(row0, tk), hh * QK_PAD:(hh + 1) * QK_PAD],
                             q[0, hh * QK_PAD:(hh + 1) * QK_PAD, :],
                             preferred_element_type=F32)

    def softmax_pv(j, hh, slot, masked):
        s = s_sc[slot]
        if masked:
            kc = lax.broadcasted_iota(jnp.int32, (tk, tq), 0) // CHUNK
            qc = lax.broadcasted_iota(jnp.int32, (tk, tq), 1) // CHUNK
            s = jnp.where(kc <= qc, s, -1e30)
        m_prev = m_sc[hh]
        m_new = jnp.maximum(m_prev, jnp.max(s, axis=0, keepdims=True))
        alpha = jnp.exp2(m_prev - m_new)
        p = jnp.exp2(s - m_new).astype(BF16)
        pv = jnp.dot(v_ref[j, hh * V_ROWS:(hh + 1) * V_ROWS, :], p,
                     preferred_element_type=F32)
        acc_sc[hh] = alpha * acc_sc[hh] + pv
        m_sc[hh] = m_new

    @pl.when(i == 0)
    def _():
        scores(0, 0, 0)

    def block(j, masked, next_j):
        for hh in range(MLA_HEADS):
            if hh + 1 < MLA_HEADS:
                scores(j, hh + 1, (hh + 1) % 2)
            elif next_j is not None:
                scores(next_j, 0, 0)
            softmax_pv(j, hh, hh % 2, masked)

    def run(j0, n):
        for t in range(n):
            block(j0 + t, False, j0 + t + 1)

    n_quads = lax.shift_right_logical(i, 2)

    def quad(q, carry):
        run(q * ATTN_UNROLL, ATTN_UNROLL)
        return carry

    lax.fori_loop(0, n_quads, quad, 0)
    done = n_quads * ATTN_UNROLL

    @pl.when((i & 2) != 0)
    def _():
        run(done, 2)

    @pl.when((i & 1) != 0)
    def _():
        run(done + (i & 2), 1)

    block(i, True, None)
    scores(0, 0, 0, q=qn_ref)

    g = g_ref[...]
    for hh in range(MLA_HEADS):
        acc = acc_sc[hh]
        o_t = acc[0:V_HEAD] / acc[V_HEAD:V_HEAD + 1]
        o_t = o_t * lax.rsqrt(jnp.mean(o_t * o_t, axis=0, keepdims=True) + EPS)
        o_ref[:, hh * V_HEAD:(hh + 1) * V_HEAD] = (
            o_t.T * g[:, hh * V_HEAD:(hh + 1) * V_HEAD]).astype(o_ref.dtype)


def _attention(q_t, k, v_t, gn_b, bsz, seq):
    tq = ATTN_TQ
    tk = v_t.shape[2]
    assert tq == tk
    nq = seq // tq
    hq = MLA_HEADS * QK_PAD
    nblk = seq // tk
    return pl.pallas_call(
        _attn_kernel,
        grid=(bsz, nq),
        in_specs=[
            pl.BlockSpec((1, hq, tq), lambda b, i: (b * nq + i, 0, 0)),
            pl.BlockSpec((1, hq, tq), lambda b, i: (jnp.minimum(b * nq + i + 1, bsz * nq - 1), 0, 0)),
            pl.BlockSpec(memory_space=pl.ANY),
            pl.BlockSpec(memory_space=pl.ANY),
            _resident(gn_b.shape),
        ],
        out_specs=pl.BlockSpec((tq, MLA_WIDTH), lambda b, i: (b * nq + i, 0)),
        out_shape=jax.ShapeDtypeStruct((bsz * seq, MLA_WIDTH), BF16),
        scratch_shapes=[
            pltpu.VMEM((seq, hq), BF16),
            pltpu.VMEM((nblk, MLA_HEADS * V_ROWS, tk), BF16),
            pltpu.SemaphoreType.DMA((2, nblk)),
            pltpu.VMEM((2, tk, tq), F32),
            pltpu.VMEM((MLA_HEADS, 1, tq), F32),
            pltpu.VMEM((MLA_HEADS, V_ROWS, tq), F32),
        ],
        compiler_params=_params(2),
        name="mla_attention",
    )(q_t, q_t, k, v_t, gn_b)


def _out_kernel(x_ref, ya_ref, yb_ref, mod_ref, wo_ref, ng_ref, w1_ref, w3_ref, w2_ref, fg_ref, o_ref):
    mod = _mod_rows(mod_ref)
    for r in range(0, x_ref.shape[0], ROW_TILE):
        rows = slice(r, r + ROW_TILE)
        y = jnp.concatenate([ya_ref[rows, :], yb_ref[rows, :]], axis=1)
        x = x_ref[rows, :] + mod[5] * jnp.dot(y, wo_ref[...], preferred_element_type=F32)
        x = _ffn_update(x, mod[6], mod[7], mod[8], ng_ref[...], w1_ref, w3_ref, w2_ref)
        o_ref[rows, :] = x * _rms_scale(x) * fg_ref[...]


def _out_ffn2(x1, ya, yb, mod, w_out, norm_g, w1, w3, w2, final_g, tiles_per_batch):
    t, d = x1.shape
    tm = FFN_STEP_ROWS
    tiles_per_batch = tiles_per_batch * ROW_TILE // tm
    return pl.pallas_call(
        _out_kernel,
        grid=(t // tm,),
        in_specs=[
            pl.BlockSpec((tm, d), lambda i: (i, 0)),
            pl.BlockSpec((tm, CONV_WIDTH), lambda i: (i, 0)),
            pl.BlockSpec((tm, MLA_WIDTH), lambda i: (i, 0)),
            pl.BlockSpec((1, 1, N_MOD * d), lambda i: (i // tiles_per_batch, 0, 0)),
            _resident(w_out.shape),
            _resident((1, d)),
            _resident(w1.shape), _resident(w3.shape), _resident(w2.shape),
            _resident((1, d)),
        ],
        out_specs=pl.BlockSpec((tm, d), lambda i: (i, 0)),
        out_shape=jax.ShapeDtypeStruct((t, d), F32),
        compiler_params=_params(),
        name="out_ffn2",
    )(x1, ya, yb, mod, w_out, norm_g, w1, w3, w2, final_g)


def kernel(x, c, positions, ada_w, ada_b, norm_ffn1_g, ffn1_w1, ffn1_w3, ffn1_w2, norm_mix_g, w_in, conv_w, q_norm_g, w_uq, kv_norm_g, w_ukv, out_norm_g, w_out, norm_ffn2_g, ffn2_w1, ffn2_w3, ffn2_w2, final_norm_g):
    bsz, seq, d = x.shape
    depth = ada_w.shape[0]
    t = bsz * seq
    tiles_per_batch = seq // ROW_TILE
    half = QK_ROPE // 2

    assert depth == 1, "only the one-layer block is implemented"
    l = 0

    cs_t, cs_row, mod, (f1w1, f1w3, f1w2) = _prologue(
        positions, c, ada_w[l], ada_b[l], [ffn1_w1[l], ffn1_w3[l], ffn1_w2[l]])

    xf = x.reshape(t, d)

    x1, (wq_t, wk, wv_t), (w_in_t, wo, f2w1, f2w3, f2w2) = _ffn1(
        xf, mod, norm_ffn1_g[l].reshape(1, d), f1w1, f1w3, f1w2, tiles_per_batch,
        w_uq[l], w_ukv[l], [w_in[l].T, w_out[l], ffn2_w1[l], ffn2_w3[l], ffn2_w2[l]])
    ya, q_t, k, v_t = _proj(
        x1, mod, norm_mix_g[l].reshape(1, d), w_in_t, conv_w[l],
        out_norm_g[l, :CONV_WIDTH].reshape(1, CONV_WIDTH), q_norm_g[l].reshape(1, Q_LORA), wq_t,
        kv_norm_g[l].reshape(1, KV_LORA), wk, wv_t, cs_row, cs_t, tiles_per_batch)
    yb = _attention(q_t, k, v_t, out_norm_g[l, CONV_WIDTH:].reshape(1, MLA_WIDTH), bsz, seq)
    xf = _out_ffn2(x1, ya, yb, mod, wo, norm_ffn2_g[l].reshape(1, d), f2w1, f2w3, f2w2,
                   final_norm_g.reshape(1, d), tiles_per_batch)
    return xf.reshape(bsz, seq, d)
```
